```python
import math
import jax, jax.numpy as jnp
from jax import lax
import numpy as np

D_MODEL = 2048
BATCH = 8
SEQ = 2048
DEPTH = 2

CHUNK = 64
N_EVEN = (DEPTH + 1) // 2
N_ODD = DEPTH // 2

D_CONV = D_MODEL // 2
D_SSM = D_MODEL // 2
CONV_WIDTH = 31
SSM_GROUP = 16
N_SSM_GROUPS = D_SSM // SSM_GROUP
SSM_STATE = 64
D_IN = 2 * D_CONV + D_SSM

N_HEADS = 16
HEAD_DIM = D_MODEL // N_HEADS
Q_BLOCK = 128

D_FF = 5632
FFN_CONV_WIDTH = 3

EPS = 1e-6

kernel_name = "hybrid_conformer_s5_stickbreak_convffn"


def rmsnorm(x, g):
    xf = x.astype(jnp.float32)
    y = xf * lax.rsqrt(jnp.mean(xf * xf, axis=-1, keepdims=True) + EPS)
    return (y * g.astype(jnp.float32)).astype(x.dtype)


def causal_depthwise_conv(x, w):
    k_width, channels = w.shape
    xp = jnp.pad(x, ((0, 0), (k_width - 1, 0), (0, 0)))
    return lax.conv_general_dilated(
        xp, w.astype(x.dtype)[:, None, :], window_strides=(1,), padding="VALID",
        dimension_numbers=("NWC", "WIO", "NWC"), feature_group_count=channels)


def conformer_conv(u_val, u_gate, w_dw, b_dw, ln_g, ln_b):
    h = u_val * jax.nn.sigmoid(u_gate)
    h = causal_depthwise_conv(h, w_dw) + b_dw.astype(h.dtype)
    hf = h.astype(jnp.float32)
    mu = jnp.mean(hf, axis=-1, keepdims=True)
    var = jnp.mean(jnp.square(hf - mu), axis=-1, keepdims=True)
    hf = (hf - mu) * lax.rsqrt(var + EPS) * ln_g.astype(jnp.float32) + ln_b.astype(jnp.float32)
    return jax.nn.silu(hf).astype(u_val.dtype)


def s5_mixer(u, lam_re, lam_im, log_step, b_re, b_im, c_re, c_im, d_skip, w_glu, b_glu):
    bsz, seq_len, _ = u.shape
    f32 = jnp.float32
    uf = u.astype(f32).reshape(bsz, seq_len, N_SSM_GROUPS, SSM_GROUP)
    lr, li = lam_re.astype(f32), lam_im.astype(f32)
    step = jnp.exp(log_step.astype(f32))[:, None]
    mag = jnp.exp(lr * step)
    lbar_re = mag * jnp.cos(li * step)
    lbar_im = mag * jnp.sin(li * step)
    num_re, num_im = lbar_re - 1.0, lbar_im
    den = lr * lr + li * li
    f_re = (num_re * lr + num_im * li) / den
    f_im = (num_im * lr - num_re * li) / den
    br, bi = b_re.astype(f32), b_im.astype(f32)
    bb_re = f_re[..., None] * br - f_im[..., None] * bi
    bb_im = f_re[..., None] * bi + f_im[..., None] * br
    bu_re = jnp.einsum("blgh,gph->blgp", uf, bb_re)
    bu_im = jnp.einsum("blgh,gph->blgp", uf, bb_im)
    a_re = jnp.broadcast_to(lbar_re, (1, seq_len, N_SSM_GROUPS, SSM_STATE))
    a_im = jnp.broadcast_to(lbar_im, (1, seq_len, N_SSM_GROUPS, SSM_STATE))

    def combine(earlier, later):
        ar1, ai1, xr1, xi1 = earlier
        ar2, ai2, xr2, xi2 = later
        ar = ar2 * ar1 - ai2 * ai1
        ai = ar2 * ai1 + ai2 * ar1
        xr = ar2 * xr1 - ai2 * xi1 + xr2
        xi = ar2 * xi1 + ai2 * xr1 + xi2
        return (ar, ai, xr, xi)

    _, _, s_re, s_im = lax.associative_scan(combine, (a_re, a_im, bu_re, bu_im), axis=1)
    y = (jnp.einsum("blgp,ghp->blgh", s_re, c_re.astype(f32))
         - jnp.einsum("blgp,ghp->blgh", s_im, c_im.astype(f32))
         + d_skip.astype(f32) * uf)
    y = jax.nn.gelu(y.reshape(bsz, seq_len, D_SSM))
    y = y * jax.nn.sigmoid(y @ w_glu.astype(f32) + b_glu.astype(f32))
    return y.astype(u.dtype)


def stick_breaking_attention(q, k, v):
    seq_len = q.shape[2]
    scale = HEAD_DIM ** -0.5
    outs = []
    for start in range(0, seq_len, Q_BLOCK):
        end = start + Q_BLOCK
        qb = q[:, :, start:end]
        kb = k[:, :, :end]
        vb = v[:, :, :end]
        z = jnp.einsum("bhqd,bhkd->bhqk", qb, kb).astype(jnp.float32) * scale
        t_idx = start + jnp.arange(Q_BLOCK)[:, None]
        s_idx = jnp.arange(end)[None, :]
        before = s_idx < t_idx
        log_beta = jax.nn.log_sigmoid(z)
        log_keep = jnp.where(before, jax.nn.log_sigmoid(-z), 0.0)
        suffix = lax.cumsum(log_keep, axis=3, reverse=True) - log_keep
        w = jnp.where(before, jnp.exp(log_beta + suffix), 0.0)
        outs.append(jnp.einsum("bhqk,bhkd->bhqd", w.astype(v.dtype), vb))
    return jnp.concatenate(outs, axis=2)


def conv_ffn(x, w_up, w_dw, w_down):
    u = x @ w_up
    u = causal_depthwise_conv(u, w_dw)
    gate, val = u[..., :D_FF], u[..., D_FF:]
    return (jax.nn.silu(gate) * val) @ w_down


def setup_inputs(seed: int = 0) -> dict:
    key = jax.random.key(seed)
    ks = jax.random.split(key, 32)
    f32 = jnp.float32
    nrm = lambda k, shape, s: jax.random.normal(k, shape, f32) * s
    gain = lambda k, shape: 1.0 + 0.02 * jax.random.normal(k, shape, f32)
    G, P, H = N_SSM_GROUPS, SSM_STATE, SSM_GROUP
    n_idx = jnp.arange(P, dtype=f32)
    return {
        "x": jax.random.normal(ks[0], (BATCH, SEQ, D_MODEL), f32),
        "ln_mix_even": gain(ks[1], (N_EVEN, D_MODEL)),
        "w_in": nrm(ks[2], (N_EVEN, D_MODEL, D_IN), D_MODEL ** -0.5),
        "conv_w": nrm(ks[3], (N_EVEN, CONV_WIDTH, D_CONV), CONV_WIDTH ** -0.5),
        "conv_b": nrm(ks[4], (N_EVEN, D_CONV), 0.02),
        "conv_ln_g": gain(ks[5], (N_EVEN, D_CONV)),
        "conv_ln_b": nrm(ks[6], (N_EVEN, D_CONV), 0.02),
        "ssm_lam_re": -0.5 + 0.01 * jax.random.normal(ks[7], (N_EVEN, G, P), f32),
        "ssm_lam_im": jnp.pi * n_idx + 0.01 * jax.random.normal(ks[8], (N_EVEN, G, P), f32),
        "ssm_log_step": jax.random.uniform(ks[9], (N_EVEN, G), f32, math.log(1e-3), math.log(1e-1)),
        "ssm_b_re": nrm(ks[10], (N_EVEN, G, P, H), (2 * H) ** -0.5),
        "ssm_b_im": nrm(ks[11], (N_EVEN, G, P, H), (2 * H) ** -0.5),
        "ssm_c_re": nrm(ks[12], (N_EVEN, G, H, P), P ** -0.5),
        "ssm_c_im": nrm(ks[13], (N_EVEN, G, H, P), P ** -0.5),
        "ssm_d": nrm(ks[14], (N_EVEN, G, H), 1.0),
        "ssm_w_glu": nrm(ks[15], (N_EVEN, D_SSM, D_SSM), D_SSM ** -0.5),
        "ssm_b_glu": nrm(ks[16], (N_EVEN, D_SSM), 0.02),
        "w_out_even": nrm(ks[17], (N_EVEN, D_CONV + D_SSM, D_MODEL), (D_CONV + D_SSM) ** -0.5),
        "ln_mix_odd": gain(ks[18], (N_ODD, D_MODEL)),
        "w_qkv": nrm(ks[19], (N_ODD, D_MODEL, 3 * D_MODEL), D_MODEL ** -0.5),
        "w_o": nrm(ks[20], (N_ODD, D_MODEL, D_MODEL), D_MODEL ** -0.5),
        "ln_ffn": gain(ks[21], (DEPTH, D_MODEL)),
        "ffn_w_up": nrm(ks[22], (DEPTH, D_MODEL, 2 * D_FF), D_MODEL ** -0.5),
        "ffn_conv_w": nrm(ks[23], (DEPTH, FFN_CONV_WIDTH, 2 * D_FF), FFN_CONV_WIDTH ** -0.5),
        "ffn_w_down": nrm(ks[24], (DEPTH, D_FF, D_MODEL), D_FF ** -0.5),
        "ln_final": gain(ks[25], (D_MODEL,)),
    }


def reference(x, ln_mix_even, w_in, conv_w, conv_b, conv_ln_g, conv_ln_b,
              ssm_lam_re, ssm_lam_im, ssm_log_step, ssm_b_re, ssm_b_im,
              ssm_c_re, ssm_c_im, ssm_d, ssm_w_glu, ssm_b_glu, w_out_even,
              ln_mix_odd, w_qkv, w_o,
              ln_ffn, ffn_w_up, ffn_conv_w, ffn_w_down, ln_final):
    bsz, seq_len, _ = x.shape
    for layer in range(DEPTH):
        i = layer // 2
        if layer % 2 == 0:
            h = rmsnorm(x, ln_mix_even[i])
            u = h @ w_in[i]
            ya = conformer_conv(u[..., :D_CONV], u[..., D_CONV:2 * D_CONV],
                                conv_w[i], conv_b[i], conv_ln_g[i], conv_ln_b[i])
            yb = s5_mixer(u[..., 2 * D_CONV:], ssm_lam_re[i], ssm_lam_im[i], ssm_log_step[i],
                          ssm_b_re[i], ssm_b_im[i], ssm_c_re[i], ssm_c_im[i], ssm_d[i],
                          ssm_w_glu[i], ssm_b_glu[i])
            x = x + jnp.concatenate([ya, yb], axis=-1) @ w_out_even[i]
        else:
            h = rmsnorm(x, ln_mix_odd[i])
            qkv = (h @ w_qkv[i]).reshape(bsz, seq_len, 3, N_HEADS, HEAD_DIM)
            qkv = jnp.transpose(qkv, (2, 0, 3, 1, 4))
            o = stick_breaking_attention(qkv[0], qkv[1], qkv[2])
            o = jnp.transpose(o, (0, 2, 1, 3)).reshape(bsz, seq_len, D_MODEL)
            x = x + o @ w_o[i]
        h = rmsnorm(x, ln_ffn[layer])
        x = x + conv_ffn(h, ffn_w_up[layer], ffn_conv_w[layer], ffn_w_down[layer])
    return rmsnorm(x, ln_final)
```

```python
import functools

import jax
import jax.numpy as jnp
from jax import lax
from jax.experimental import pallas as pl
from jax.experimental.pallas import tpu as pltpu

F32 = jnp.float32
BF16 = jnp.bfloat16

EPS = 1e-6

V7X_SUBLANES = 8
V7X_LANES = 128
V7X_VMEM_BYTES = 64 * 1024 * 1024

N_HEADS = 16
N_SLABS = 8


def _vmem_limit(nbytes):
    return int(min(max(nbytes * 3 // 2, 16 * 1024 * 1024), V7X_VMEM_BYTES * 7 // 8))


def _params(semantics, nbytes):
    return pltpu.CompilerParams(dimension_semantics=semantics, vmem_limit_bytes=_vmem_limit(nbytes))


def _dot(a, b):
    return jnp.dot(a, b, preferred_element_type=F32)


def _rmsnorm(x, g):
    ms = jnp.mean(x * x, axis=-1, keepdims=True)
    return x * lax.rsqrt(ms + EPS) * g


def _sigmoid(x):
    return 1.0 / (1.0 + jnp.exp(-x))


def _zoh_kernel(lr_ref, li_ref, ls_ref, br_ref, bi_ref, are_ref, aim_ref, bbr_ref, bbi_ref):
    lr = lr_ref[...]
    li = li_ref[...]
    step = jnp.exp(ls_ref[...])
    mag = jnp.exp(lr * step)
    lbar_re = mag * jnp.cos(li * step)
    lbar_im = mag * jnp.sin(li * step)
    num_re = lbar_re - 1.0
    num_im = lbar_im
    den = lr * lr + li * li
    f_re = (num_re * lr + num_im * li) / den
    f_im = (num_im * lr - num_re * li) / den
    br = br_ref[...]
    bi = bi_ref[...]
    are_ref[...] = lbar_re
    aim_ref[...] = lbar_im
    bbr_ref[...] = f_re * br - f_im * bi
    bbi_ref[...] = f_re * bi + f_im * br


def _zoh(lam_re, lam_im, log_step, b_re, b_im):
    g, p = lam_re.shape
    h = b_re.shape[-1]
    rep = lambda a: jnp.broadcast_to(a[:, None, :], (g, h, p)).reshape(g * h, p)
    args = (rep(lam_re), rep(lam_im), rep(jnp.broadcast_to(log_step[:, None], (g, p))),
            jnp.transpose(b_re, (0, 2, 1)).reshape(g * h, p),
            jnp.transpose(b_im, (0, 2, 1)).reshape(g * h, p))
    out = pl.pallas_call(
        _zoh_kernel,
        out_shape=[jax.ShapeDtypeStruct((g * h, p), F32)] * 4,
        name="zoh",
    )(*args)
    are, aim, bbr, bbi = [o.reshape(g, h, p) for o in out]
    return are[:, 0], aim[:, 0], bbr, bbi


def _norm_matmul_kernel(x_ref, g_ref, w_ref, *rest, has_scale):
    if has_scale:
        s_ref, o_ref, h_ref = rest
    else:
        o_ref, h_ref = rest

    @pl.when(pl.program_id(2) == 0)
    def _():
        h_ref[...] = _rmsnorm(x_ref[...], g_ref[...]).astype(BF16)

    acc = _dot(h_ref[...], w_ref[...])
    if has_scale:
        acc = acc * s_ref[...]
    o_ref[...] = acc.astype(o_ref.dtype)


def _norm_matmul(x2d, x_index, gain, w, out_shape, out_index, *, grid_bt, tm, tn, col_scale=None, name):
    d, n = w.shape
    nb, nt = grid_bt
    in_specs = [
        pl.BlockSpec((tm, d), lambda b, t, j: x_index(b, t)),
        pl.BlockSpec((1, d), lambda b, t, j: (0, 0)),
        pl.BlockSpec((d, tn), lambda b, t, j: (0, j)),
    ]
    args = [x2d, gain.reshape(1, d), w]
    if col_scale is not None:
        in_specs.append(pl.BlockSpec((1, tn), lambda b, t, j: (0, j)))
        args.append(col_scale.reshape(1, n))
    out_bytes = jnp.dtype(out_shape.dtype).itemsize
    nbytes = 2 * tm * d * 4 + tm * d * 2 + 2 * d * tn * 2 + 2 * tm * tn * out_bytes + tm * tn * 4
    return pl.pallas_call(
        functools.partial(_norm_matmul_kernel, has_scale=col_scale is not None),
        grid=(nb, nt, n // tn),
        in_specs=in_specs,
        out_specs=pl.BlockSpec((tm, tn), out_index),
        out_shape=out_shape,
        scratch_shapes=[pltpu.VMEM((tm, d), BF16)],
        compiler_params=_params(("parallel", "parallel", "arbitrary"), nbytes),
        name=name,
    )(*args)


def _res_matmul_kernel(*refs, n_a):
    a_refs = refs[:n_a]
    w_refs = refs[n_a:2 * n_a]
    res_ref, o_ref = refs[2 * n_a:]
    acc = res_ref[...]
    for a_ref, w_ref in zip(a_refs, w_refs):
        acc = acc + _dot(a_ref[...], w_ref[...])
    o_ref[...] = acc


def _res_matmul(a_list, a_index, w, res2d, res_index, out_shape, out_index, *, ka, grid_bt, tm, tn, name):
    n_a = len(a_list)
    assert n_a * ka == w.shape[0]
    n = w.shape[1]
    nb, nt = grid_bt
    in_specs = [pl.BlockSpec((tm, ka), lambda b, t, j: a_index(b, t)) for _ in a_list]
    in_specs += [pl.BlockSpec((ka, tn), lambda b, t, j, i=i: (i, j)) for i in range(n_a)]
    in_specs += [pl.BlockSpec((tm, tn), res_index)]
    nbytes = 2 * tm * n_a * ka * 2 + 2 * n_a * ka * tn * 2 + 4 * tm * tn * 4 + tm * tn * 4
    return pl.pallas_call(
        functools.partial(_res_matmul_kernel, n_a=n_a),
        grid=(nb, nt, n // tn),
        in_specs=in_specs,
        out_specs=pl.BlockSpec((tm, tn), out_index),
        out_shape=out_shape,
        compiler_params=_params(("parallel", "parallel", "arbitrary"), nbytes),
        name=name,
    )(*a_list, *([w] * n_a), res2d)


CONV_T = 64
CONV_HALO = 32
CONV_TB = 16
CONV_CW = 128


def _conformer_kernel(val_ref, gate_ref, hval_ref, hgate_ref, wb_ref, cb_ref, lng_ref, lnb_ref, o_ref,
                      hx_ref, c_ref, *, taps):
    t_len, _, ch = val_ref.shape
    keep = (pl.program_id(0) > 0).astype(F32)
    hx_ref[0:CONV_HALO] = hval_ref[...] * _sigmoid(hgate_ref[...]) * keep
    hx_ref[CONV_HALO:CONV_HALO + t_len] = val_ref[...] * _sigmoid(gate_ref[...])

    lead = CONV_HALO - (taps - 1)
    for c0 in range(0, ch, CONV_CW):
        def body(tb, carry, c0=c0):
            t0 = pl.multiple_of(tb * CONV_TB, CONV_TB)
            acc = jnp.zeros((CONV_TB, val_ref.shape[1], CONV_CW), F32)
            for k in range(taps):
                acc = acc + wb_ref[k, :, c0:c0 + CONV_CW][None] * hx_ref[pl.ds(t0 + lead + k, CONV_TB), :, c0:c0 + CONV_CW]
            c_ref[pl.ds(t0, CONV_TB), :, c0:c0 + CONV_CW] = acc
            return carry
        lax.fori_loop(0, t_len // CONV_TB, body, 0)

    h = c_ref[...] + cb_ref[...]
    mu = jnp.mean(h, axis=-1, keepdims=True)
    dlt = h - mu
    var = jnp.mean(dlt * dlt, axis=-1, keepdims=True)
    y = dlt * lax.rsqrt(var + EPS) * lng_ref[...] + lnb_ref[...]
    o_ref[...] = (y * _sigmoid(y)).astype(o_ref.dtype)


def _conformer(u3, conv_w, conv_b, ln_g, ln_b):
    seq, bsz, _ = u3.shape
    taps, ch = conv_w.shape
    assert taps - 1 <= CONV_HALO and CONV_T % CONV_HALO == 0 and seq % CONV_T == 0
    wb = jnp.broadcast_to(conv_w[:, None, :], (taps, bsz, ch))
    ratio = CONV_T // CONV_HALO
    halo_idx = lambda c: (lambda t: (jnp.maximum(t * ratio - 1, 0), 0, c))
    row = lambda a: a.reshape(1, 1, ch)
    full3 = lambda t: (0, 0, 0)
    nbytes = (2 * 2 * (CONV_T + CONV_HALO) * bsz * ch * 4 + 2 * taps * bsz * ch * 4
              + (2 * CONV_T + CONV_HALO) * bsz * ch * 4 + 2 * CONV_T * bsz * ch * 2 + 4 * CONV_T * bsz * ch * 4)
    return pl.pallas_call(
        functools.partial(_conformer_kernel, taps=taps),
        grid=(seq // CONV_T,),
        in_specs=[
            pl.BlockSpec((CONV_T, bsz, ch), lambda t: (t, 0, 0)),
            pl.BlockSpec((CONV_T, bsz, ch), lambda t: (t, 0, 1)),
            pl.BlockSpec((CONV_HALO, bsz, ch), halo_idx(0)),
            pl.BlockSpec((CONV_HALO, bsz, ch), halo_idx(1)),
            pl.BlockSpec((taps, bsz, ch), full3),
            pl.BlockSpec((1, 1, ch), full3),
            pl.BlockSpec((1, 1, ch), full3),
            pl.BlockSpec((1, 1, ch), full3),
        ],
        out_specs=pl.BlockSpec((CONV_T, bsz, ch), lambda t: (t, 0, 0)),
        out_shape=jax.ShapeDtypeStruct((seq, bsz, ch), BF16),
        scratch_shapes=[pltpu.VMEM((CONV_HALO + CONV_T, bsz, ch), F32), pltpu.VMEM((CONV_T, bsz, ch), F32)],
        compiler_params=_params(("parallel",), nbytes),
        name="conformer",
    )(u3, u3, u3, u3, wb, row(conv_b), row(ln_g), row(ln_b))


S5_T = 32


def _s5_kernel(u_ref, wb_ref, lbar_ref, wcr_ref, wci_ref, d_ref, wg_ref, bg_ref, o_ref, bu_ref, st_ref):
    t_len, bsz, ch = u_ref.shape
    rows = t_len * bsz
    n_slabs, _, width = lbar_ref.shape
    half = width // 2
    cin = wb_ref.shape[1]

    @pl.when(pl.program_id(0) == 0)
    def _():
        st_ref[...] = jnp.zeros(st_ref.shape, F32)

    u = u_ref[...].reshape(rows, ch)
    ub = u.astype(BF16)
    for j in range(n_slabs):
        bu_ref[j] = _dot(ub[:, j * cin:(j + 1) * cin], wb_ref[j])

    def scan_slab(j, carry):
        a = lbar_ref[j]
        ar, ai = a[:, :half], a[:, half:]
        s = st_ref[j]
        xr, xi = s[:, :half], s[:, half:]
        for t in range(t_len):
            b = bu_ref[j, t * bsz:(t + 1) * bsz, :]
            xr, xi = ar * xr - ai * xi + b[:, :half], ar * xi + ai * xr + b[:, half:]
            bu_ref[j, t * bsz:(t + 1) * bsz, :] = jnp.concatenate([xr, xi], axis=-1)
        st_ref[j] = jnp.concatenate([xr, xi], axis=-1)
        return carry
    lax.fori_loop(0, n_slabs, scan_slab, 0)

    ys = []
    for j in range(n_slabs):
        s = bu_ref[j]
        ys.append(_dot(s[:, :half].astype(BF16), wcr_ref[j]) - _dot(s[:, half:].astype(BF16), wci_ref[j]))
    y = jnp.concatenate(ys, axis=-1) + d_ref[...] * u
    y = jax.nn.gelu(y)
    gate = _sigmoid(_dot(y.astype(BF16), wg_ref[...]) + bg_ref[...])
    o_ref[...] = (y * gate).reshape(t_len, bsz, ch).astype(o_ref.dtype)


def _s5(u3, cblk, lbar_re, lbar_im, bb_re, bb_im, c_re, c_im, d_skip, w_glu, b_glu):
    seq, bsz, _ = u3.shape
    g, h, p = bb_re.shape
    ch = g * h
    gs = g // N_SLABS
    half = gs * p
    eye = jnp.eye(gs, dtype=F32)

    def place_b(bb):
        return jnp.einsum("jghp,gk->jghkp", bb.reshape(N_SLABS, gs, h, p), eye).reshape(N_SLABS, gs * h, half)

    def place_c(c):
        return jnp.einsum("jghp,gk->jgpkh", c.reshape(N_SLABS, gs, h, p), eye).reshape(N_SLABS, half, gs * h)

    wb = jnp.concatenate([place_b(bb_re), place_b(bb_im)], axis=-1).astype(BF16)
    wcr = place_c(c_re).astype(BF16)
    wci = place_c(c_im).astype(BF16)
    lbar = jnp.concatenate([lbar_re.reshape(N_SLABS, half), lbar_im.reshape(N_SLABS, half)], axis=-1)
    lbar = jnp.broadcast_to(lbar[:, None, :], (N_SLABS, bsz, 2 * half))
    rows = S5_T * bsz
    full2 = lambda t: (0, 0)
    full3 = lambda t: (0, 0, 0)
    nbytes = (2 * rows * ch * 4 + 2 * rows * ch * 2 + N_SLABS * rows * 2 * half * 4
              + 2 * (wb.size + wcr.size + wci.size + w_glu.size) * 2 + 2 * lbar.size * 4 + 6 * rows * ch * 4)
    return pl.pallas_call(
        _s5_kernel,
        grid=(seq // S5_T,),
        in_specs=[
            pl.BlockSpec((S5_T, bsz, ch), lambda t: (t, 0, cblk)),
            pl.BlockSpec(wb.shape, full3),
            pl.BlockSpec(lbar.shape, full3),
            pl.BlockSpec(wcr.shape, full3),
            pl.BlockSpec(wci.shape, full3),
            pl.BlockSpec((1, ch), full2),
            pl.BlockSpec(w_glu.shape, full2),
            pl.BlockSpec((1, ch), full2),
        ],
        out_specs=pl.BlockSpec((S5_T, bsz, ch), lambda t: (t, 0, 0)),
        out_shape=jax.ShapeDtypeStruct((seq, bsz, ch), BF16),
        scratch_shapes=[pltpu.VMEM((N_SLABS, rows, 2 * half), F32), pltpu.VMEM((N_SLABS, bsz, 2 * half), F32)],
        compiler_params=_params(("arbitrary",), nbytes),
        name="s5",
    )(u3, wb, lbar, wcr, wci, d_skip.reshape(1, ch), w_glu, b_glu.reshape(1, ch))


FFN_TM = 512
FFN_TC = 512


def _ffn_kernel(x_ref, halo_ref, g_ref, wg_ref, wv_ref, cg_ref, cv_ref, wd_ref, o_ref, h_ref, *, bsz, taps):
    tm = x_ref.shape[0]
    halo = halo_ref.shape[0]

    @pl.when(pl.program_id(1) == 0)
    def _():
        x = x_ref[...]
        keep = (pl.program_id(0) > 0).astype(F32)
        h_ref[0:halo, :] = (_rmsnorm(halo_ref[...], g_ref[...]) * keep).astype(BF16)
        h_ref[halo:halo + tm, :] = _rmsnorm(x, g_ref[...]).astype(BF16)
        o_ref[...] = x

    h = h_ref[...]

    def conv(u, cw):
        out = cw[0:1, :] * u[0:tm]
        for k in range(1, taps):
            out = out + cw[k:k + 1, :] * u[k * bsz:k * bsz + tm]
        return out

    gate = conv(_dot(h, wg_ref[...]), cg_ref[...])
    val = conv(_dot(h, wv_ref[...]), cv_ref[...])
    act = (gate * _sigmoid(gate) * val).astype(BF16)
    o_ref[...] += _dot(act, wd_ref[...])


def _ffn(x2d, bsz, gain, w_up, conv_w, w_down):
    n_rows, d = x2d.shape
    taps = conv_w.shape[0]
    d_ff = w_down.shape[0]
    halo = (taps - 1) * bsz
    assert bsz % V7X_SUBLANES == 0 and FFN_TM % halo == 0 and n_rows % FFN_TM == 0 and d_ff % FFN_TC == 0
    nc = d_ff // FFN_TC
    ratio = FFN_TM // halo
    nbytes = (4 * FFN_TM * d * 4 + 2 * halo * d * 4 + (FFN_TM + halo) * d * 2 + 2 * 3 * d * FFN_TC * 2
              + 4 * (FFN_TM + halo) * FFN_TC * 4)
    return pl.pallas_call(
        functools.partial(_ffn_kernel, bsz=bsz, taps=taps),
        grid=(n_rows // FFN_TM, nc),
        in_specs=[
            pl.BlockSpec((FFN_TM, d), lambda i, c: (i, 0)),
            pl.BlockSpec((halo, d), lambda i, c: (jnp.maximum(i * ratio - 1, 0), 0)),
            pl.BlockSpec((1, d), lambda i, c: (0, 0)),
            pl.BlockSpec((d, FFN_TC), lambda i, c: (0, c)),
            pl.BlockSpec((d, FFN_TC), lambda i, c: (0, nc + c)),
            pl.BlockSpec((taps, FFN_TC), lambda i, c: (0, c)),
            pl.BlockSpec((taps, FFN_TC), lambda i, c: (0, nc + c)),
            pl.BlockSpec((FFN_TC, d), lambda i, c: (c, 0)),
        ],
        out_specs=pl.BlockSpec((FFN_TM, d), lambda i, c: (i, 0)),
        out_shape=jax.ShapeDtypeStruct((n_rows, d), F32),
        scratch_shapes=[pltpu.VMEM((halo + FFN_TM, d), BF16)],
        compiler_params=_params(("parallel", "arbitrary"), nbytes),
        name="ffn",
    )(x2d, x2d, gain.reshape(1, d), w_up, w_up, conv_w, conv_w, w_down)


ATT_TQ = 256
ATT_TK = 256


def _attn_kernel(q_ref, k_ref, v_ref, m_ref, o_ref, acc_ref, carry_ref):
    tq = q_ref.shape[0]
    tk = m_ref.shape[0]
    qi = pl.program_id(2)
    q = q_ref[...]
    tri = m_ref[...]

    acc_ref[...] = jnp.zeros(acc_ref.shape, F32)
    carry_ref[...] = jnp.zeros(carry_ref.shape, F32)

    def block(kb, diagonal):
        ks = pl.multiple_of(kb * tk, tk)
        z = lax.dot_general(q, k_ref[pl.ds(ks, tk), :], (((1,), (1,)), ((), ())), preferred_element_type=F32)
        log_beta = jnp.minimum(z, 0.0) - jnp.log(1.0 + jnp.exp(-jnp.abs(z)))
        log_keep = log_beta - z
        if diagonal:
            before = (lax.broadcasted_iota(jnp.int32, (tq, tk), 1)
                      < lax.broadcasted_iota(jnp.int32, (tq, tk), 0))
            log_keep = jnp.where(before, log_keep, 0.0)
        hi = log_keep.astype(BF16)
        lo = (log_keep - hi.astype(F32)).astype(BF16)
        suffix = _dot(hi, tri) + _dot(lo, tri)
        w = jnp.exp(log_beta + suffix)
        if diagonal:
            w = jnp.where(before, w, 0.0)
        pv = _dot(w.astype(BF16), v_ref[pl.ds(ks, tk), :])
        carry = carry_ref[...]
        acc_ref[...] += jnp.exp(carry) * pv
        carry_ref[...] = carry + jnp.sum(log_keep, axis=-1, keepdims=True)

    block(qi, True)

    def body(i, c):
        block(qi - 1 - i, False)
        return c
    lax.fori_loop(0, qi, body, 0)
    o_ref[...] = acc_ref[...].astype(o_ref.dtype)


def _attention(qkv, bsz, seq):
    d = qkv.shape[1] // 3
    dh = d // N_HEADS
    assert ATT_TQ == ATT_TK and seq % ATT_TQ == 0
    nq = seq // ATT_TQ
    tri = (jnp.arange(ATT_TK)[:, None] > jnp.arange(ATT_TK)[None, :]).astype(BF16)
    nbytes = (2 * ATT_TQ * dh * 2 + 2 * 2 * seq * dh * 2 + 2 * ATT_TK * ATT_TK * 2 + 2 * ATT_TQ * dh * 2
              + ATT_TQ * (dh + V7X_LANES) * 4 + 10 * ATT_TQ * ATT_TK * 4)
    return pl.pallas_call(
        _attn_kernel,
        grid=(bsz, N_HEADS, nq),
        in_specs=[
            pl.BlockSpec((ATT_TQ, dh), lambda b, h, i: (b * nq + i, h)),
            pl.BlockSpec((seq, dh), lambda b, h, i: (b, N_HEADS + h)),
            pl.BlockSpec((seq, dh), lambda b, h, i: (b, 2 * N_HEADS + h)),
            pl.BlockSpec((ATT_TK, ATT_TK), lambda b, h, i: (0, 0)),
        ],
        out_specs=pl.BlockSpec((ATT_TQ, dh), lambda b, h, i: (b * nq + i, h)),
        out_shape=jax.ShapeDtypeStruct((bsz * seq, d), BF16),
        scratch_shapes=[pltpu.VMEM((ATT_TQ, dh), F32), pltpu.VMEM((ATT_TQ, 1), F32)],
        compiler_params=_params(("parallel", "parallel", "arbitrary"), nbytes),
        name="attention",
    )(qkv, qkv, qkv, tri)


def _final_norm_kernel(x_ref, g_ref, o_ref):
    o_ref[...] = _rmsnorm(x_ref[...], g_ref[...])


def _final_norm(x_tm, gain, bsz, seq, tm):
    d = gain.shape[0]
    nt = seq // tm
    return pl.pallas_call(
        _final_norm_kernel,
        grid=(bsz, nt),
        in_specs=[pl.BlockSpec((tm, d), lambda b, t: (t, b)), pl.BlockSpec((1, d), lambda b, t: (0, 0))],
        out_specs=pl.BlockSpec((tm, d), lambda b, t: (b * nt + t, 0)),
        out_shape=jax.ShapeDtypeStruct((bsz * seq, d), F32),
        compiler_params=_params(("parallel", "parallel"), 4 * tm * d * 4),
        name="final_norm",
    )(x_tm, gain.reshape(1, d))


PROJ_TM = 1024
PROJ_TN = 1024


def kernel(x, ln_mix_even, w_in, conv_w, conv_b, conv_ln_g, conv_ln_b, ssm_lam_re, ssm_lam_im, ssm_log_step, ssm_b_re, ssm_b_im, ssm_c_re, ssm_c_im, ssm_d, ssm_w_glu, ssm_b_glu, w_out_even, ln_mix_odd, w_qkv, w_o, ln_ffn, ffn_w_up, ffn_conv_w, ffn_w_down, ln_final):
    bsz, seq, d = x.shape
    depth = ln_ffn.shape[0]
    assert depth == 2 and ln_mix_even.shape[0] == 1 and ln_mix_odd.shape[0] == 1
    tm = min(PROJ_TM, seq)
    tn = min(PROJ_TN, d)
    nt = seq // tm
    grid_bt = (bsz, nt)
    d_conv = conv_w.shape[-1]
    d_ssm = ssm_w_glu.shape[-1]
    d_in = w_in.shape[-1]
    assert d_conv == d_ssm and d_in == 2 * d_conv + d_ssm and d_conv % tn == 0

    bm_rows = lambda b, t: (b * nt + t, 0)
    tm_rows = lambda b, t: (t, b)
    x_bm = x.reshape(bsz * seq, d)

    u = _norm_matmul(
        x_bm, bm_rows, ln_mix_even[0], w_in[0].astype(BF16),
        jax.ShapeDtypeStruct((seq, bsz * d_in), F32), lambda b, t, j: (t, b * (d_in // tn) + j),
        grid_bt=grid_bt, tm=tm, tn=tn, name="in_proj")
    u3 = u.reshape(seq, bsz, d_in)
    ya = _conformer(u3, conv_w[0], conv_b[0], conv_ln_g[0], conv_ln_b[0])
    lbar_re, lbar_im, bb_re, bb_im = _zoh(ssm_lam_re[0], ssm_lam_im[0], ssm_log_step[0], ssm_b_re[0], ssm_b_im[0])
    yb = _s5(u3, 2, lbar_re, lbar_im, bb_re, bb_im, ssm_c_re[0], ssm_c_im[0], ssm_d[0],
             ssm_w_glu[0].astype(BF16), ssm_b_glu[0])
    x_tm = _res_matmul(
        [ya.reshape(seq, bsz * d_conv), yb.reshape(seq, bsz * d_ssm)], tm_rows, w_out_even[0].astype(BF16),
        x_bm, lambda b, t, j: (b * nt + t, j),
        jax.ShapeDtypeStruct((seq, bsz * d), F32), lambda b, t, j: (t, b * (d // tn) + j),
        ka=d_conv, grid_bt=grid_bt, tm=tm, tn=tn, name="out_proj")
    x_tm = _ffn(x_tm.reshape(seq * bsz, d), bsz, ln_ffn[0], ffn_w_up[0].astype(BF16), ffn_conv_w[0],
                ffn_w_down[0].astype(BF16)).reshape(seq, bsz * d)

    dh = d // N_HEADS
    col_scale = jnp.concatenate([jnp.full((d,), dh ** -0.5, F32), jnp.ones((2 * d,), F32)])
    qkv = _norm_matmul(
        x_tm, tm_rows, ln_mix_odd[0], w_qkv[0].astype(BF16),
        jax.ShapeDtypeStruct((bsz * seq, 3 * d), BF16), lambda b, t, j: (b * nt + t, j),
        grid_bt=grid_bt, tm=tm, tn=tn, col_scale=col_scale, name="qkv_proj")
    o = _attention(qkv, bsz, seq)
    x_tm = _res_matmul(
        [o], bm_rows, w_o[0].astype(BF16),
        x_tm, lambda b, t, j: (t, b * (d // tn) + j),
        jax.ShapeDtypeStruct((seq, bsz * d), F32), lambda b, t, j: (t, b * (d // tn) + j),
        ka=d, grid_bt=grid_bt, tm=tm, tn=tn, name="o_proj")
    x_tm = _ffn(x_tm.reshape(seq * bsz, d), bsz, ln_ffn[1], ffn_w_up[1].astype(BF16), ffn_conv_w[1],
                ffn_w_down[1].astype(BF16)).reshape(seq, bsz * d)

    return _final_norm(x_tm, ln_final, bsz, seq, tm).reshape(bsz, seq, d)
```

```python
import functools

import jax
import jax.numpy as jnp
from jax import lax
from jax.experimental import pallas as pl
from jax.experimental.pallas import tpu as pltpu

F32 = jnp.float32
BF16 = jnp.bfloat16

EPS = 1e-6

V7X_SUBLANES = 8
V7X_LANES = 128
V7X_VMEM_BYTES = 64 * 1024 * 1024

N_HEADS = 16
N_SLABS = 8


def _vmem_limit(nbytes):
    return int(min(max(nbytes * 3 // 2, 16 * 1024 * 1024), V7X_VMEM_BYTES * 7 // 8))


def _params(semantics, nbytes):
    return pltpu.CompilerParams(dimension_semantics=semantics, vmem_limit_bytes=_vmem_limit(nbytes))


def _dot(a, b):
    return jnp.dot(a, b, preferred_element_type=F32)


def _rmsnorm(x, g):
    ms = jnp.mean(x * x, axis=-1, keepdims=True)
    return x * lax.rsqrt(ms + EPS) * g


def _sigmoid(x):
    return 1.0 / (1.0 + jnp.exp(-x))


def _to_time_major(src, dst_ref, row0):
    bsz, t_len, ch = src.shape
    for b in range(bsz):
        for s in range(ch // V7X_LANES):
            dst_ref[s, pl.ds(row0 + b, t_len, stride=bsz), :] = src[b, :, s * V7X_LANES:(s + 1) * V7X_LANES]


def _from_time_major(src_ref, o_ref):
    bsz, t_len, ch = o_ref.shape
    for b in range(bsz):
        for s in range(ch // V7X_LANES):
            o_ref[b, :, s * V7X_LANES:(s + 1) * V7X_LANES] = (
                src_ref[s, pl.ds(b, t_len, stride=bsz), :].astype(o_ref.dtype))


def _zoh_kernel(lr_ref, li_ref, ls_ref, br_ref, bi_ref, are_ref, aim_ref, bbr_ref, bbi_ref):
    lr = lr_ref[...]
    li = li_ref[...]
    step = jnp.exp(ls_ref[...])
    mag = jnp.exp(lr * step)
    lbar_re = mag * jnp.cos(li * step)
    lbar_im = mag * jnp.sin(li * step)
    num_re = lbar_re - 1.0
    num_im = lbar_im
    den = lr * lr + li * li
    f_re = (num_re * lr + num_im * li) / den
    f_im = (num_im * lr - num_re * li) / den
    br = br_ref[...]
    bi = bi_ref[...]
    are_ref[...] = lbar_re
    aim_ref[...] = lbar_im
    bbr_ref[...] = f_re * br - f_im * bi
    bbi_ref[...] = f_re * bi + f_im * br


def _zoh(lam_re, lam_im, log_step, b_re, b_im):
    g, p = lam_re.shape
    h = b_re.shape[-1]
    rep = lambda a: jnp.broadcast_to(a[:, None, :], (g, h, p)).reshape(g * h, p)
    args = (rep(lam_re), rep(lam_im), rep(jnp.broadcast_to(log_step[:, None], (g, p))),
            jnp.transpose(b_re, (0, 2, 1)).reshape(g * h, p),
            jnp.transpose(b_im, (0, 2, 1)).reshape(g * h, p))
    out = pl.pallas_call(
        _zoh_kernel,
        out_shape=[jax.ShapeDtypeStruct((g * h, p), F32)] * 4,
        name="zoh",
    )(*args)
    are, aim, bbr, bbi = [o.reshape(g, h, p) for o in out]
    return are[:, 0], aim[:, 0], bbr, bbi


def _norm_matmul_kernel(x_ref, g_ref, w_ref, *rest, has_scale):
    if has_scale:
        s_ref, o_ref, h_ref = rest
    else:
        o_ref, h_ref = rest

    @pl.when(pl.program_id(1) == 0)
    def _():
        h_ref[...] = _rmsnorm(x_ref[...], g_ref[...]).astype(BF16)

    acc = _dot(h_ref[...], w_ref[...])
    if has_scale:
        acc = acc * s_ref[...]
    o_ref[...] = acc.astype(o_ref.dtype)


def _norm_matmul(x2d, gain, w, out_dtype, *, tm, tn, col_scale=None, name):
    n_rows = x2d.shape[0]
    d, n = w.shape
    in_specs = [
        pl.BlockSpec((tm, d), lambda i, j: (i, 0)),
        pl.BlockSpec((1, d), lambda i, j: (0, 0)),
        pl.BlockSpec((d, tn), lambda i, j: (0, j)),
    ]
    args = [x2d, gain.reshape(1, d), w]
    if col_scale is not None:
        in_specs.append(pl.BlockSpec((1, tn), lambda i, j: (0, j)))
        args.append(col_scale.reshape(1, n))
    out_bytes = jnp.dtype(out_dtype).itemsize
    nbytes = 2 * tm * d * 4 + tm * d * 2 + 2 * d * tn * 2 + 2 * tm * tn * out_bytes + tm * tn * 4
    return pl.pallas_call(
        functools.partial(_norm_matmul_kernel, has_scale=col_scale is not None),
        grid=(n_rows // tm, n // tn),
        in_specs=in_specs,
        out_specs=pl.BlockSpec((tm, tn), lambda i, j: (i, j)),
        out_shape=jax.ShapeDtypeStruct((n_rows, n), out_dtype),
        scratch_shapes=[pltpu.VMEM((tm, d), BF16)],
        compiler_params=_params(("parallel", "arbitrary"), nbytes),
        name=name,
    )(*args)


def _res_matmul_kernel(*refs, n_a):
    a_refs = refs[:n_a]
    w_refs = refs[n_a:2 * n_a]
    res_ref, o_ref = refs[2 * n_a:]
    acc = res_ref[...]
    for a_ref, w_ref in zip(a_refs, w_refs):
        acc = acc + _dot(a_ref[...], w_ref[...])
    o_ref[...] = acc


def _res_matmul(a_list, w, res2d, *, tm, tn, name):
    n_a = len(a_list)
    n_rows, ka = a_list[0].shape
    assert n_a * ka == w.shape[0]
    n = w.shape[1]
    in_specs = [pl.BlockSpec((tm, ka), lambda i, j: (i, 0)) for _ in a_list]
    in_specs += [pl.BlockSpec((ka, tn), lambda i, j, r=r: (r, j)) for r in range(n_a)]
    in_specs += [pl.BlockSpec((tm, tn), lambda i, j: (i, j))]
    nbytes = 2 * tm * n_a * ka * 2 + 2 * n_a * ka * tn * 2 + 4 * tm * tn * 4 + tm * tn * 4
    return pl.pallas_call(
        functools.partial(_res_matmul_kernel, n_a=n_a),
        grid=(n_rows // tm, n // tn),
        in_specs=in_specs,
        out_specs=pl.BlockSpec((tm, tn), lambda i, j: (i, j)),
        out_shape=jax.ShapeDtypeStruct((n_rows, n), F32),
        compiler_params=_params(("parallel", "arbitrary"), nbytes),
        name=name,
    )(*a_list, *([w] * n_a), res2d)


CONV_T = 64
CONV_HALO = 32
CONV_TB = 16


def _conformer_kernel(val_ref, gate_ref, hval_ref, hgate_ref, wb_ref, cb_ref, lng_ref, lnb_ref, o_ref,
                      hx_ref, c_ref, *, taps):
    bsz, t_len, ch = val_ref.shape
    n_slab = ch // V7X_LANES
    keep = (pl.program_id(0) > 0).astype(F32)
    _to_time_major(hval_ref[...] * _sigmoid(hgate_ref[...]) * keep, hx_ref, 0)
    _to_time_major(val_ref[...] * _sigmoid(gate_ref[...]), hx_ref, CONV_HALO * bsz)

    lead = CONV_HALO - (taps - 1)
    for s in range(n_slab):
        def body(tb, carry, s=s):
            t0 = tb * CONV_TB
            acc = jnp.zeros((CONV_TB, bsz, V7X_LANES), F32)
            for k in range(taps):
                r0 = pl.multiple_of((t0 + lead + k) * bsz, bsz)
                rows = hx_ref[s, pl.ds(r0, CONV_TB * bsz), :].reshape(CONV_TB, bsz, V7X_LANES)
                acc = acc + wb_ref[k, :, s * V7X_LANES:(s + 1) * V7X_LANES][None] * rows
            c_ref[s, pl.ds(pl.multiple_of(t0 * bsz, bsz), CONV_TB * bsz), :] = acc.reshape(CONV_TB * bsz, V7X_LANES)
            return carry
        lax.fori_loop(0, t_len // CONV_TB, body, 0)

    h = jnp.concatenate([c_ref[s] for s in range(n_slab)], axis=-1) + cb_ref[...]
    mu = jnp.mean(h, axis=-1, keepdims=True)
    dlt = h - mu
    var = jnp.mean(dlt * dlt, axis=-1, keepdims=True)
    y = dlt * lax.rsqrt(var + EPS) * lng_ref[...] + lnb_ref[...]
    y = y * _sigmoid(y)
    for s in range(n_slab):
        c_ref[s] = y[:, s * V7X_LANES:(s + 1) * V7X_LANES]
    _from_time_major(c_ref, o_ref)


def _conformer(u3, conv_w, conv_b, ln_g, ln_b):
    bsz, seq, _ = u3.shape
    taps, ch = conv_w.shape
    assert taps - 1 <= CONV_HALO and CONV_T % CONV_HALO == 0 and seq % CONV_T == 0 and ch % V7X_LANES == 0
    wb = jnp.broadcast_to(conv_w[:, None, :], (taps, bsz, ch))
    ratio = CONV_T // CONV_HALO
    halo_idx = lambda c: (lambda t: (0, jnp.maximum(t * ratio - 1, 0), c))
    row = lambda a: a.reshape(1, ch)
    full2 = lambda t: (0, 0)
    nbytes = (2 * 2 * (CONV_T + CONV_HALO) * bsz * ch * 4 + 2 * taps * bsz * ch * 4
              + (2 * CONV_T + CONV_HALO) * bsz * ch * 4 + 2 * CONV_T * bsz * ch * 2 + 6 * CONV_T * bsz * ch * 4)
    return pl.pallas_call(
        functools.partial(_conformer_kernel, taps=taps),
        grid=(seq // CONV_T,),
        in_specs=[
            pl.BlockSpec((bsz, CONV_T, ch), lambda t: (0, t, 0)),
            pl.BlockSpec((bsz, CONV_T, ch), lambda t: (0, t, 1)),
            pl.BlockSpec((bsz, CONV_HALO, ch), halo_idx(0)),
            pl.BlockSpec((bsz, CONV_HALO, ch), halo_idx(1)),
            pl.BlockSpec((taps, bsz, ch), lambda t: (0, 0, 0)),
            pl.BlockSpec((1, ch), full2),
            pl.BlockSpec((1, ch), full2),
            pl.BlockSpec((1, ch), full2),
        ],
        out_specs=pl.BlockSpec((bsz, CONV_T, ch), lambda t: (0, t, 0)),
        out_shape=jax.ShapeDtypeStruct((bsz, seq, ch), BF16),
        scratch_shapes=[pltpu.VMEM((ch // V7X_LANES, (CONV_HALO + CONV_T) * bsz, V7X_LANES), F32),
                        pltpu.VMEM((ch // V7X_LANES, CONV_T * bsz, V7X_LANES), F32)],
        compiler_params=_params(("parallel",), nbytes),
        name="conformer",
    )(u3, u3, u3, u3, wb, row(conv_b), row(ln_g), row(ln_b))


S5_T = 32


def _s5_kernel(u_ref, wb_ref, lbar_ref, wcr_ref, wci_ref, d_ref, wg_ref, bg_ref, o_ref, ut_ref, bu_ref, st_ref):
    bsz, t_len, ch = u_ref.shape
    rows = t_len * bsz
    n_slabs, _, width = lbar_ref.shape
    half = width // 2
    cin = wb_ref.shape[1]

    @pl.when(pl.program_id(0) == 0)
    def _():
        st_ref[...] = jnp.zeros(st_ref.shape, F32)

    _to_time_major(u_ref[...], ut_ref, 0)
    u = jnp.concatenate([ut_ref[s] for s in range(ch // V7X_LANES)], axis=-1)
    ub = u.astype(BF16)
    for j in range(n_slabs):
        bu_ref[j] = _dot(ub[:, j * cin:(j + 1) * cin], wb_ref[j])

    def scan_slab(j, carry):
        a = lbar_ref[j]
        ar, ai = a[:, :half], a[:, half:]
        s = st_ref[j]
        xr, xi = s[:, :half], s[:, half:]
        for t in range(t_len):
            b = bu_ref[j, t * bsz:(t + 1) * bsz, :]
            xr, xi = ar * xr - ai * xi + b[:, :half], ar * xi + ai * xr + b[:, half:]
            bu_ref[j, t * bsz:(t + 1) * bsz, :] = jnp.concatenate([xr, xi], axis=-1)
        st_ref[j] = jnp.concatenate([xr, xi], axis=-1)
        return carry
    lax.fori_loop(0, n_slabs, scan_slab, 0)

    ys = []
    for j in range(n_slabs):
        s = bu_ref[j]
        ys.append(_dot(s[:, :half].astype(BF16), wcr_ref[j]) - _dot(s[:, half:].astype(BF16), wci_ref[j]))
    y = jnp.concatenate(ys, axis=-1) + d_ref[...] * u
    y = jax.nn.gelu(y)
    gate = _sigmoid(_dot(y.astype(BF16), wg_ref[...]) + bg_ref[...])
    y = y * gate
    for s in range(ch // V7X_LANES):
        ut_ref[s] = y[:, s * V7X_LANES:(s + 1) * V7X_LANES]
    _from_time_major(ut_ref, o_ref)


def _s5(u3, cblk, lbar_re, lbar_im, bb_re, bb_im, c_re, c_im, d_skip, w_glu, b_glu):
    bsz, seq, _ = u3.shape
    g, h, p = bb_re.shape
    ch = g * h
    gs = g // N_SLABS
    half = gs * p
    eye = jnp.eye(gs, dtype=F32)

    def place_b(bb):
        return jnp.einsum("jghp,gk->jghkp", bb.reshape(N_SLABS, gs, h, p), eye).reshape(N_SLABS, gs * h, half)

    def place_c(c):
        return jnp.einsum("jghp,gk->jgpkh", c.reshape(N_SLABS, gs, h, p), eye).reshape(N_SLABS, half, gs * h)

    wb = jnp.concatenate([place_b(bb_re), place_b(bb_im)], axis=-1).astype(BF16)
    wcr = place_c(c_re).astype(BF16)
    wci = place_c(c_im).astype(BF16)
    lbar = jnp.concatenate([lbar_re.reshape(N_SLABS, half), lbar_im.reshape(N_SLABS, half)], axis=-1)
    lbar = jnp.broadcast_to(lbar[:, None, :], (N_SLABS, bsz, 2 * half))
    rows = S5_T * bsz
    full2 = lambda t: (0, 0)
    full3 = lambda t: (0, 0, 0)
    nbytes = (2 * rows * ch * 4 + 2 * rows * ch * 2 + rows * ch * 4 + N_SLABS * rows * 2 * half * 4
              + 2 * (wb.size + wcr.size + wci.size + w_glu.size) * 2 + 2 * lbar.size * 4 + 6 * rows * ch * 4)
    return pl.pallas_call(
        _s5_kernel,
        grid=(seq // S5_T,),
        in_specs=[
            pl.BlockSpec((bsz, S5_T, ch), lambda t: (0, t, cblk)),
            pl.BlockSpec(wb.shape, full3),
            pl.BlockSpec(lbar.shape, full3),
            pl.BlockSpec(wcr.shape, full3),
            pl.BlockSpec(wci.shape, full3),
            pl.BlockSpec((1, ch), full2),
            pl.BlockSpec(w_glu.shape, full2),
            pl.BlockSpec((1, ch), full2),
        ],
        out_specs=pl.BlockSpec((bsz, S5_T, ch), lambda t: (0, t, 0)),
        out_shape=jax.ShapeDtypeStruct((bsz, seq, ch), BF16),
        scratch_shapes=[pltpu.VMEM((ch // V7X_LANES, rows, V7X_LANES), F32),
                        pltpu.VMEM((N_SLABS, rows, 2 * half), F32),
                        pltpu.VMEM((N_SLABS, bsz, 2 * half), F32)],
        compiler_params=_params(("arbitrary",), nbytes),
        name="s5",
    )(u3, wb, lbar, wcr, wci, d_skip.reshape(1, ch), w_glu, b_glu.reshape(1, ch))


FFN_TM = 512
FFN_TC = 512


def _ffn_kernel(x_ref, halo_ref, g_ref, wg_ref, wv_ref, cg_ref, cv_ref, wd_ref, *rest, taps, tiles_per_seq, final):
    if final:
        fg_ref, o_ref, h_ref = rest
    else:
        o_ref, h_ref = rest
    tm = x_ref.shape[0]
    halo = halo_ref.shape[0]
    c = pl.program_id(1)

    @pl.when(c == 0)
    def _():
        x = x_ref[...]
        keep = (pl.program_id(0) % tiles_per_seq > 0).astype(F32)
        h_ref[0:halo, :] = (_rmsnorm(halo_ref[...], g_ref[...]) * keep).astype(BF16)
        h_ref[halo:halo + tm, :] = _rmsnorm(x, g_ref[...]).astype(BF16)
        o_ref[...] = x

    h = h_ref[...]
    lead = halo - (taps - 1)

    def conv(u, cw):
        out = cw[0:1, :] * u[lead:lead + tm]
        for k in range(1, taps):
            out = out + cw[k:k + 1, :] * u[lead + k:lead + k + tm]
        return out

    gate = conv(_dot(h, wg_ref[...]), cg_ref[...])
    val = conv(_dot(h, wv_ref[...]), cv_ref[...])
    act = (gate * _sigmoid(gate) * val).astype(BF16)
    o_ref[...] += _dot(act, wd_ref[...])

    if final:
        @pl.when(c == pl.num_programs(1) - 1)
        def _():
            o_ref[...] = _rmsnorm(o_ref[...], fg_ref[...])


def _ffn(x2d, seq, gain, w_up, conv_w, w_down, final_gain=None):
    n_rows, d = x2d.shape
    taps = conv_w.shape[0]
    d_ff = w_down.shape[0]
    halo = V7X_SUBLANES
    assert taps - 1 <= halo and seq % FFN_TM == 0 and d_ff % FFN_TC == 0
    nc = d_ff // FFN_TC
    ratio = FFN_TM // halo
    final = final_gain is not None
    in_specs = [
        pl.BlockSpec((FFN_TM, d), lambda i, c: (i, 0)),
        pl.BlockSpec((halo, d), lambda i, c: (jnp.maximum(i * ratio - 1, 0), 0)),
        pl.BlockSpec((1, d), lambda i, c: (0, 0)),
        pl.BlockSpec((d, FFN_TC), lambda i, c: (0, c)),
        pl.BlockSpec((d, FFN_TC), lambda i, c: (0, nc + c)),
        pl.BlockSpec((taps, FFN_TC), lambda i, c: (0, c)),
        pl.BlockSpec((taps, FFN_TC), lambda i, c: (0, nc + c)),
        pl.BlockSpec((FFN_TC, d), lambda i, c: (c, 0)),
    ]
    args = [x2d, x2d, gain.reshape(1, d), w_up, w_up, conv_w, conv_w, w_down]
    if final:
        in_specs.append(pl.BlockSpec((1, d), lambda i, c: (0, 0)))
        args.append(final_gain.reshape(1, d))
    nbytes = (4 * FFN_TM * d * 4 + 2 * halo * d * 4 + (FFN_TM + halo) * d * 2 + 2 * 3 * d * FFN_TC * 2
              + 6 * (FFN_TM + halo) * FFN_TC * 4)
    return pl.pallas_call(
        functools.partial(_ffn_kernel, taps=taps, tiles_per_seq=seq // FFN_TM, final=final),
        grid=(n_rows // FFN_TM, nc),
        in_specs=in_specs,
        out_specs=pl.BlockSpec((FFN_TM, d), lambda i, c: (i, 0)),
        out_shape=jax.ShapeDtypeStruct((n_rows, d), F32),
        scratch_shapes=[pltpu.VMEM((halo + FFN_TM, d), BF16)],
        compiler_params=_params(("parallel", "arbitrary"), nbytes),
        name="ffn",
    )(*args)


ATT_T = 256
ATT_G = 2
LOG2_E = 1.4426950408889634
F32_EXP2_UNDERFLOW = -150.0


def _attn_kernel(q_ref, k_ref, v_ref, tri_ref, o_ref, acc_ref, carry_ref):
    blk = tri_ref.shape[0]
    n_g = acc_ref.shape[0]
    dh = acc_ref.shape[1]
    qi = pl.program_id(2)
    tri = tri_ref[...]

    acc_ref[...] = jnp.zeros(acc_ref.shape, F32)
    carry_ref[...] = jnp.zeros(carry_ref.shape, F32)

    def process(chains):
        def rows(kb):
            return pl.ds(pl.multiple_of(kb * blk, blk), blk)
        cols = lambda g: slice(g * dh, (g + 1) * dh)
        zs = [lax.dot_general(k_ref[rows(kb), cols(g)], q_ref[:, cols(g)], (((1,), (1,)), ((), ())),
                              preferred_element_type=F32) for g, kb, _ in chains]
        before = (lax.broadcasted_iota(jnp.int32, (blk, blk), 0) < lax.broadcasted_iota(jnp.int32, (blk, blk), 1))
        parts = []
        for (_, _, diagonal), z in zip(chains, zs):
            log_beta = jnp.minimum(z, 0.0) - jnp.log(1.0 + jnp.exp2(-jnp.abs(z))) * LOG2_E
            log_keep = log_beta - z
            if diagonal:
                log_keep = jnp.where(before, log_keep, 0.0)
            hi = log_keep.astype(BF16)
            lo = (log_keep - hi.astype(F32)).astype(BF16)
            parts.append((log_beta, log_keep, hi, lo))
        sufs = [_dot(tri, hi) + _dot(tri, lo) for _, _, hi, lo in parts]
        pvs = []
        for (g, kb, diagonal), (log_beta, _, _, _), suffix in zip(chains, parts, sufs):
            w = jnp.exp2(log_beta + suffix)
            if diagonal:
                w = jnp.where(before, w, 0.0)
            pvs.append(lax.dot_general(v_ref[rows(kb), cols(g)], w.astype(BF16), (((0,), (0,)), ((), ())),
                                       preferred_element_type=F32))
        for (g, _, _), (_, log_keep, _, _), pv in zip(chains, parts, pvs):
            carry = carry_ref[g]
            acc_ref[g] += jnp.exp2(carry) * pv
            carry_ref[g] = carry + jnp.sum(log_keep, axis=0, keepdims=True)

    diag_chains = [(g, qi, True) for g in range(n_g)]

    @pl.when(qi == 0)
    def _():
        process(diag_chains)

    @pl.when(qi > 0)
    def _():
        process(diag_chains + [(g, qi - 1, False) for g in range(n_g)])

    def cond(state):
        kb, max_carry = state
        return jnp.logical_and(kb >= 0, max_carry > F32_EXP2_UNDERFLOW)

    def body(state):
        kb, _ = state
        process([(g, kb, False) for g in range(n_g)])
        return kb - 1, jnp.max(carry_ref[...])

    lax.while_loop(cond, body, (qi - 2, jnp.max(carry_ref[...])))
    for g in range(n_g):
        o_ref[:, g * dh:(g + 1) * dh] = acc_ref[g].T.astype(o_ref.dtype)


def _attention(qkv, bsz, seq):
    d = qkv.shape[1] // 3
    dh = d // N_HEADS
    assert seq % ATT_T == 0 and N_HEADS % ATT_G == 0
    nq = seq // ATT_T
    hg = N_HEADS // ATT_G
    gw = ATT_G * dh
    idx = jnp.arange(ATT_T)
    tri = (idx[None, :] > idx[:, None]).astype(BF16)
    nbytes = (2 * ATT_T * gw * 2 + 2 * 2 * seq * gw * 2 + 2 * ATT_T * ATT_T * 2 + 2 * ATT_T * gw * 2
              + ATT_G * (dh + V7X_SUBLANES) * ATT_T * 4 + 2 * ATT_G * 10 * ATT_T * ATT_T * 4)
    return pl.pallas_call(
        _attn_kernel,
        grid=(bsz, hg, nq),
        in_specs=[
            pl.BlockSpec((ATT_T, gw), lambda b, h, i: (b * nq + i, h)),
            pl.BlockSpec((seq, gw), lambda b, h, i: (b, hg + h)),
            pl.BlockSpec((seq, gw), lambda b, h, i: (b, 2 * hg + h)),
            pl.BlockSpec((ATT_T, ATT_T), lambda b, h, i: (0, 0)),
        ],
        out_specs=pl.BlockSpec((ATT_T, gw), lambda b, h, i: (b * nq + i, h)),
        out_shape=jax.ShapeDtypeStruct((bsz * seq, d), BF16),
        scratch_shapes=[pltpu.VMEM((ATT_G, dh, ATT_T), F32), pltpu.VMEM((ATT_G, 1, ATT_T), F32)],
        compiler_params=_params(("parallel", "parallel", "arbitrary"), nbytes),
        name="attention",
    )(qkv, qkv, qkv, tri)


PROJ_TM = 1024
PROJ_TN = 1024


def kernel(x, ln_mix_even, w_in, conv_w, conv_b, conv_ln_g, conv_ln_b, ssm_lam_re, ssm_lam_im, ssm_log_step, ssm_b_re, ssm_b_im, ssm_c_re, ssm_c_im, ssm_d, ssm_w_glu, ssm_b_glu, w_out_even, ln_mix_odd, w_qkv, w_o, ln_ffn, ffn_w_up, ffn_conv_w, ffn_w_down, ln_final):
    bsz, seq, d = x.shape
    depth = ln_ffn.shape[0]
    assert depth == 2 and ln_mix_even.shape[0] == 1 and ln_mix_odd.shape[0] == 1
    tm = min(PROJ_TM, seq)
    tn = min(PROJ_TN, d)
    d_conv = conv_w.shape[-1]
    d_ssm = ssm_w_glu.shape[-1]
    d_in = w_in.shape[-1]
    assert d_conv == d_ssm and d_in == 2 * d_conv + d_ssm
    x2d = x.reshape(bsz * seq, d)

    u = _norm_matmul(x2d, ln_mix_even[0], w_in[0].astype(BF16), F32, tm=tm, tn=tn, name="in_proj")
    u3 = u.reshape(bsz, seq, d_in)
    ya = _conformer(u3, conv_w[0], conv_b[0], conv_ln_g[0], conv_ln_b[0])
    lbar_re, lbar_im, bb_re, bb_im = _zoh(ssm_lam_re[0], ssm_lam_im[0], ssm_log_step[0], ssm_b_re[0], ssm_b_im[0])
    yb = _s5(u3, 2, lbar_re, lbar_im, bb_re, bb_im, ssm_c_re[0], ssm_c_im[0], ssm_d[0],
             ssm_w_glu[0].astype(BF16), ssm_b_glu[0])
    x2d = _res_matmul([ya.reshape(bsz * seq, d_conv), yb.reshape(bsz * seq, d_ssm)], w_out_even[0].astype(BF16),
                      x2d, tm=tm, tn=tn, name="out_proj")
    x2d = _ffn(x2d, seq, ln_ffn[0], ffn_w_up[0].astype(BF16), ffn_conv_w[0], ffn_w_down[0].astype(BF16))

    dh = d // N_HEADS
    col_scale = jnp.concatenate([jnp.full((d,), dh ** -0.5 * LOG2_E, F32), jnp.ones((2 * d,), F32)])
    qkv = _norm_matmul(x2d, ln_mix_odd[0], w_qkv[0].astype(BF16), BF16, tm=tm, tn=tn, col_scale=col_scale,
                       name="qkv_proj")
    o = _attention(qkv, bsz, seq)
    x2d = _res_matmul([o], w_o[0].astype(BF16), x2d, tm=tm, tn=tn, name="o_proj")
    x2d = _ffn(x2d, seq, ln_ffn[1], ffn_w_up[1].astype(BF16), ffn_conv_w[1], ffn_w_down[1].astype(BF16),
               final_gain=ln_final)
    return x2d.reshape(bsz, seq, d)
```

```python
import functools

import jax
import jax.numpy as jnp
from jax import lax
from jax.experimental import pallas as pl
from jax.experimental.pallas import tpu as pltpu

F32 = jnp.float32
BF16 = jnp.bfloat16

EPS = 1e-6

V7X_SUBLANES = 8
V7X_LANES = 128
V7X_VMEM_BYTES = 64 * 1024 * 1024

N_HEADS = 16
N_SLABS = 8


def _vmem_limit(nbytes):
    return int(min(max(nbytes * 3 // 2, 16 * 1024 * 1024), V7X_VMEM_BYTES * 7 // 8))


def _params(semantics, nbytes):
    return pltpu.CompilerParams(dimension_semantics=semantics, vmem_limit_bytes=_vmem_limit(nbytes))


def _dot(a, b):
    return jnp.dot(a, b, preferred_element_type=F32)


def _rmsnorm(x, g):
    ms = jnp.mean(x * x, axis=-1, keepdims=True)
    return x * lax.rsqrt(ms + EPS) * g


def _sigmoid(x):
    return 1.0 / (1.0 + jnp.exp(-x))


def _to_time_major(src, dst_ref, row0):
    bsz, t_len, ch = src.shape
    for b in range(bsz):
        for s in range(ch // V7X_LANES):
            dst_ref[s, pl.ds(row0 + b, t_len, stride=bsz), :] = src[b, :, s * V7X_LANES:(s + 1) * V7X_LANES]


def _from_time_major(src_ref, o_ref):
    bsz, t_len, ch = o_ref.shape
    for b in range(bsz):
        for s in range(ch // V7X_LANES):
            o_ref[b, :, s * V7X_LANES:(s + 1) * V7X_LANES] = (
                src_ref[s, pl.ds(b, t_len, stride=bsz), :].astype(o_ref.dtype))


def _zoh_kernel(lr_ref, li_ref, ls_ref, br_ref, bi_ref, are_ref, aim_ref, bbr_ref, bbi_ref):
    lr = lr_ref[...]
    li = li_ref[...]
    step = jnp.exp(ls_ref[...])
    mag = jnp.exp(lr * step)
    lbar_re = mag * jnp.cos(li * step)
    lbar_im = mag * jnp.sin(li * step)
    num_re = lbar_re - 1.0
    num_im = lbar_im
    den = lr * lr + li * li
    f_re = (num_re * lr + num_im * li) / den
    f_im = (num_im * lr - num_re * li) / den
    br = br_ref[...]
    bi = bi_ref[...]
    are_ref[...] = lbar_re
    aim_ref[...] = lbar_im
    bbr_ref[...] = f_re * br - f_im * bi
    bbi_ref[...] = f_re * bi + f_im * br


def _zoh(lam_re, lam_im, log_step, b_re, b_im):
    g, p = lam_re.shape
    h = b_re.shape[-1]
    rep = lambda a: jnp.broadcast_to(a[:, None, :], (g, h, p)).reshape(g * h, p)
    args = (rep(lam_re), rep(lam_im), rep(jnp.broadcast_to(log_step[:, None], (g, p))),
            jnp.transpose(b_re, (0, 2, 1)).reshape(g * h, p),
            jnp.transpose(b_im, (0, 2, 1)).reshape(g * h, p))
    out = pl.pallas_call(
        _zoh_kernel,
        out_shape=[jax.ShapeDtypeStruct((g * h, p), F32)] * 4,
        name="zoh",
    )(*args)
    are, aim, bbr, bbi = [o.reshape(g, h, p) for o in out]
    return are[:, 0], aim[:, 0], bbr, bbi


def _norm_matmul_kernel(x_ref, g_ref, w_ref, *rest, has_scale):
    if has_scale:
        s_ref, o_ref, h_ref = rest
    else:
        o_ref, h_ref = rest

    @pl.when(pl.program_id(1) == 0)
    def _():
        h_ref[...] = _rmsnorm(x_ref[...], g_ref[...]).astype(BF16)

    acc = _dot(h_ref[...], w_ref[...])
    if has_scale:
        acc = acc * s_ref[...]
    o_ref[...] = acc.astype(o_ref.dtype)


def _norm_matmul(x2d, gain, w, out_dtype, *, tm, tn, col_scale=None, name):
    n_rows = x2d.shape[0]
    d, n = w.shape
    in_specs = [
        pl.BlockSpec((tm, d), lambda i, j: (i, 0)),
        pl.BlockSpec((1, d), lambda i, j: (0, 0)),
        pl.BlockSpec((d, tn), lambda i, j: (0, j)),
    ]
    args = [x2d, gain.reshape(1, d), w]
    if col_scale is not None:
        in_specs.append(pl.BlockSpec((1, tn), lambda i, j: (0, j)))
        args.append(col_scale.reshape(1, n))
    out_bytes = jnp.dtype(out_dtype).itemsize
    nbytes = 2 * tm * d * 4 + tm * d * 2 + 2 * d * tn * 2 + 2 * tm * tn * out_bytes + tm * tn * 4
    return pl.pallas_call(
        functools.partial(_norm_matmul_kernel, has_scale=col_scale is not None),
        grid=(n_rows // tm, n // tn),
        in_specs=in_specs,
        out_specs=pl.BlockSpec((tm, tn), lambda i, j: (i, j)),
        out_shape=jax.ShapeDtypeStruct((n_rows, n), out_dtype),
        scratch_shapes=[pltpu.VMEM((tm, d), BF16)],
        compiler_params=_params(("parallel", "arbitrary"), nbytes),
        name=name,
    )(*args)


def _res_matmul_kernel(*refs, n_a):
    a_refs = refs[:n_a]
    w_refs = refs[n_a:2 * n_a]
    res_ref, o_ref = refs[2 * n_a:]
    acc = res_ref[...]
    for a_ref, w_ref in zip(a_refs, w_refs):
        acc = acc + _dot(a_ref[...], w_ref[...])
    o_ref[...] = acc


def _res_matmul(a_list, w, res2d, *, tm, tn, name):
    n_a = len(a_list)
    n_rows, ka = a_list[0].shape
    assert n_a * ka == w.shape[0]
    n = w.shape[1]
    in_specs = [pl.BlockSpec((tm, ka), lambda i, j: (i, 0)) for _ in a_list]
    in_specs += [pl.BlockSpec((ka, tn), lambda i, j, r=r: (r, j)) for r in range(n_a)]
    in_specs += [pl.BlockSpec((tm, tn), lambda i, j: (i, j))]
    nbytes = 2 * tm * n_a * ka * 2 + 2 * n_a * ka * tn * 2 + 4 * tm * tn * 4 + tm * tn * 4
    return pl.pallas_call(
        functools.partial(_res_matmul_kernel, n_a=n_a),
        grid=(n_rows // tm, n // tn),
        in_specs=in_specs,
        out_specs=pl.BlockSpec((tm, tn), lambda i, j: (i, j)),
        out_shape=jax.ShapeDtypeStruct((n_rows, n), F32),
        compiler_params=_params(("parallel", "arbitrary"), nbytes),
        name=name,
    )(*a_list, *([w] * n_a), res2d)


CONV_T = 64
CONV_HALO = 32
CONV_TB = 16


def _conformer_kernel(val_ref, gate_ref, hval_ref, hgate_ref, wb_ref, cb_ref, lng_ref, lnb_ref, o_ref,
                      hx_ref, c_ref, *, taps):
    bsz, t_len, ch = val_ref.shape
    n_slab = ch // V7X_LANES
    keep = (pl.program_id(0) > 0).astype(F32)
    _to_time_major(hval_ref[...] * _sigmoid(hgate_ref[...]) * keep, hx_ref, 0)
    _to_time_major(val_ref[...] * _sigmoid(gate_ref[...]), hx_ref, CONV_HALO * bsz)

    lead = CONV_HALO - (taps - 1)
    for s in range(n_slab):
        def body(tb, carry, s=s):
            t0 = tb * CONV_TB
            acc = jnp.zeros((CONV_TB, bsz, V7X_LANES), F32)
            for k in range(taps):
                r0 = pl.multiple_of((t0 + lead + k) * bsz, bsz)
                rows = hx_ref[s, pl.ds(r0, CONV_TB * bsz), :].reshape(CONV_TB, bsz, V7X_LANES)
                acc = acc + wb_ref[k, :, s * V7X_LANES:(s + 1) * V7X_LANES][None] * rows
            c_ref[s, pl.ds(pl.multiple_of(t0 * bsz, bsz), CONV_TB * bsz), :] = acc.reshape(CONV_TB * bsz, V7X_LANES)
            return carry
        lax.fori_loop(0, t_len // CONV_TB, body, 0)

    h = jnp.concatenate([c_ref[s] for s in range(n_slab)], axis=-1) + cb_ref[...]
    mu = jnp.mean(h, axis=-1, keepdims=True)
    dlt = h - mu
    var = jnp.mean(dlt * dlt, axis=-1, keepdims=True)
    y = dlt * lax.rsqrt(var + EPS) * lng_ref[...] + lnb_ref[...]
    y = y * _sigmoid(y)
    for s in range(n_slab):
        c_ref[s] = y[:, s * V7X_LANES:(s + 1) * V7X_LANES]
    _from_time_major(c_ref, o_ref)


def _conformer(u3, conv_w, conv_b, ln_g, ln_b):
    bsz, seq, _ = u3.shape
    taps, ch = conv_w.shape
    assert taps - 1 <= CONV_HALO and CONV_T % CONV_HALO == 0 and seq % CONV_T == 0 and ch % V7X_LANES == 0
    wb = jnp.broadcast_to(conv_w[:, None, :], (taps, bsz, ch))
    ratio = CONV_T // CONV_HALO
    halo_idx = lambda c: (lambda t: (0, jnp.maximum(t * ratio - 1, 0), c))
    row = lambda a: a.reshape(1, ch)
    full2 = lambda t: (0, 0)
    nbytes = (2 * 2 * (CONV_T + CONV_HALO) * bsz * ch * 4 + 2 * taps * bsz * ch * 4
              + (2 * CONV_T + CONV_HALO) * bsz * ch * 4 + 2 * CONV_T * bsz * ch * 2 + 6 * CONV_T * bsz * ch * 4)
    return pl.pallas_call(
        functools.partial(_conformer_kernel, taps=taps),
        grid=(seq // CONV_T,),
        in_specs=[
            pl.BlockSpec((bsz, CONV_T, ch), lambda t: (0, t, 0)),
            pl.BlockSpec((bsz, CONV_T, ch), lambda t: (0, t, 1)),
            pl.BlockSpec((bsz, CONV_HALO, ch), halo_idx(0)),
            pl.BlockSpec((bsz, CONV_HALO, ch), halo_idx(1)),
            pl.BlockSpec((taps, bsz, ch), lambda t: (0, 0, 0)),
            pl.BlockSpec((1, ch), full2),
            pl.BlockSpec((1, ch), full2),
            pl.BlockSpec((1, ch), full2),
        ],
        out_specs=pl.BlockSpec((bsz, CONV_T, ch), lambda t: (0, t, 0)),
        out_shape=jax.ShapeDtypeStruct((bsz, seq, ch), BF16),
        scratch_shapes=[pltpu.VMEM((ch // V7X_LANES, (CONV_HALO + CONV_T) * bsz, V7X_LANES), F32),
                        pltpu.VMEM((ch // V7X_LANES, CONV_T * bsz, V7X_LANES), F32)],
        compiler_params=_params(("parallel",), nbytes),
        name="conformer",
    )(u3, u3, u3, u3, wb, row(conv_b), row(ln_g), row(ln_b))


S5_T = 32


def _s5_kernel(u_ref, wb_ref, lbar_ref, wcr_ref, wci_ref, d_ref, wg_ref, bg_ref, o_ref, ut_ref, bu_ref, st_ref):
    bsz, t_len, ch = u_ref.shape
    rows = t_len * bsz
    n_slabs, _, width = lbar_ref.shape
    half = width // 2
    cin = wb_ref.shape[1]

    @pl.when(pl.program_id(0) == 0)
    def _():
        st_ref[...] = jnp.zeros(st_ref.shape, F32)

    _to_time_major(u_ref[...], ut_ref, 0)
    u = jnp.concatenate([ut_ref[s] for s in range(ch // V7X_LANES)], axis=-1)
    ub = u.astype(BF16)
    for j in range(n_slabs):
        bu_ref[j] = _dot(ub[:, j * cin:(j + 1) * cin], wb_ref[j])

    def scan_slab(j, carry):
        a = lbar_ref[j]
        ar, ai = a[:, :half], a[:, half:]
        s = st_ref[j]
        xr, xi = s[:, :half], s[:, half:]
        for t in range(t_len):
            b = bu_ref[j, t * bsz:(t + 1) * bsz, :]
            xr, xi = ar * xr - ai * xi + b[:, :half], ar * xi + ai * xr + b[:, half:]
            bu_ref[j, t * bsz:(t + 1) * bsz, :] = jnp.concatenate([xr, xi], axis=-1)
        st_ref[j] = jnp.concatenate([xr, xi], axis=-1)
        return carry
    lax.fori_loop(0, n_slabs, scan_slab, 0)

    ys = []
    for j in range(n_slabs):
        s = bu_ref[j]
        ys.append(_dot(s[:, :half].astype(BF16), wcr_ref[j]) - _dot(s[:, half:].astype(BF16), wci_ref[j]))
    y = jnp.concatenate(ys, axis=-1) + d_ref[...] * u
    y = jax.nn.gelu(y)
    gate = _sigmoid(_dot(y.astype(BF16), wg_ref[...]) + bg_ref[...])
    y = y * gate
    for s in range(ch // V7X_LANES):
        ut_ref[s] = y[:, s * V7X_LANES:(s + 1) * V7X_LANES]
    _from_time_major(ut_ref, o_ref)


def _s5(u3, cblk, lbar_re, lbar_im, bb_re, bb_im, c_re, c_im, d_skip, w_glu, b_glu):
    bsz, seq, _ = u3.shape
    g, h, p = bb_re.shape
    ch = g * h
    gs = g // N_SLABS
    half = gs * p
    eye = jnp.eye(gs, dtype=F32)

    def place_b(bb):
        return jnp.einsum("jghp,gk->jghkp", bb.reshape(N_SLABS, gs, h, p), eye).reshape(N_SLABS, gs * h, half)

    def place_c(c):
        return jnp.einsum("jghp,gk->jgpkh", c.reshape(N_SLABS, gs, h, p), eye).reshape(N_SLABS, half, gs * h)

    wb = jnp.concatenate([place_b(bb_re), place_b(bb_im)], axis=-1).astype(BF16)
    wcr = place_c(c_re).astype(BF16)
    wci = place_c(c_im).astype(BF16)
    lbar = jnp.concatenate([lbar_re.reshape(N_SLABS, half), lbar_im.reshape(N_SLABS, half)], axis=-1)
    lbar = jnp.broadcast_to(lbar[:, None, :], (N_SLABS, bsz, 2 * half))
    rows = S5_T * bsz
    full2 = lambda t: (0, 0)
    full3 = lambda t: (0, 0, 0)
    nbytes = (2 * rows * ch * 4 + 2 * rows * ch * 2 + rows * ch * 4 + N_SLABS * rows * 2 * half * 4
              + 2 * (wb.size + wcr.size + wci.size + w_glu.size) * 2 + 2 * lbar.size * 4 + 6 * rows * ch * 4)
    return pl.pallas_call(
        _s5_kernel,
        grid=(seq // S5_T,),
        in_specs=[
            pl.BlockSpec((bsz, S5_T, ch), lambda t: (0, t, cblk)),
            pl.BlockSpec(wb.shape, full3),
            pl.BlockSpec(lbar.shape, full3),
            pl.BlockSpec(wcr.shape, full3),
            pl.BlockSpec(wci.shape, full3),
            pl.BlockSpec((1, ch), full2),
            pl.BlockSpec(w_glu.shape, full2),
            pl.BlockSpec((1, ch), full2),
        ],
        out_specs=pl.BlockSpec((bsz, S5_T, ch), lambda t: (0, t, 0)),
        out_shape=jax.ShapeDtypeStruct((bsz, seq, ch), BF16),
        scratch_shapes=[pltpu.VMEM((ch // V7X_LANES, rows, V7X_LANES), F32),
                        pltpu.VMEM((N_SLABS, rows, 2 * half), F32),
                        pltpu.VMEM((N_SLABS, bsz, 2 * half), F32)],
        compiler_params=_params(("arbitrary",), nbytes),
        name="s5",
    )(u3, wb, lbar, wcr, wci, d_skip.reshape(1, ch), w_glu, b_glu.reshape(1, ch))


FFN_TM = 512
FFN_TC = 512
FFN_ROW_SPLIT = 2


def _ffn_kernel(x_ref, halo_ref, g_ref, wg_ref, wv_ref, cg_ref, cv_ref, wd_ref, *rest, taps, tiles_per_seq, final):
    if final:
        fg_ref, o_ref, h_ref, act_a, act_b = rest
    else:
        o_ref, h_ref, act_a, act_b = rest
    acts = (act_a, act_b)
    tm = x_ref.shape[0]
    halo = halo_ref.shape[0]
    c = pl.program_id(1)
    last = pl.num_programs(1) - 1
    lead = halo - (taps - 1)

    rs = tm // FFN_ROW_SPLIT

    def conv(u, cw):
        out = cw[0:1, :] * u[lead:lead + rs]
        for k in range(1, taps):
            out = out + cw[k:k + 1, :] * u[lead + k:lead + k + rs]
        return out

    def up():
        parts = []
        for s in range(FFN_ROW_SPLIT):
            h = h_ref[s * rs:(s + 1) * rs + halo, :]
            parts.append((_dot(h, wg_ref[...]), _dot(h, wv_ref[...])))
        return parts

    def gate_into(act_ref, parts):
        for s, (ug, uv) in enumerate(parts):
            gate = conv(ug, cg_ref[...])
            act_ref[s * rs:(s + 1) * rs, :] = (gate * _sigmoid(gate) * conv(uv, cv_ref[...])).astype(BF16)

    def down(act_ref):
        o_ref[...] += _dot(act_ref[...], wd_ref[...])

    @pl.when(c == 0)
    def _():
        x = x_ref[...]
        keep = (pl.program_id(0) % tiles_per_seq > 0).astype(F32)
        h_ref[0:halo, :] = (_rmsnorm(halo_ref[...], g_ref[...]) * keep).astype(BF16)
        h_ref[halo:halo + tm, :] = _rmsnorm(x, g_ref[...]).astype(BF16)
        o_ref[...] = x
        gate_into(acts[0], up())

    for parity in (0, 1):
        @pl.when(jnp.logical_and(jnp.logical_and(c > 0, c < last), c % 2 == parity))
        def _(parity=parity):
            parts = up()
            down(acts[1 - parity])
            gate_into(acts[parity], parts)

    for parity in (0, 1):
        @pl.when(jnp.logical_and(c == last, (c - 1) % 2 == parity))
        def _(parity=parity):
            down(acts[parity])
            if final:
                o_ref[...] = _rmsnorm(o_ref[...], fg_ref[...])


def _ffn(x2d, seq, gain, w_up, conv_w, w_down, final_gain=None):
    n_rows, d = x2d.shape
    taps = conv_w.shape[0]
    d_ff = w_down.shape[0]
    halo = V7X_SUBLANES
    assert taps - 1 <= halo and seq % FFN_TM == 0 and d_ff % FFN_TC == 0
    nc = d_ff // FFN_TC
    ratio = FFN_TM // halo
    final = final_gain is not None
    up_c = lambda c: jnp.minimum(c, nc - 1)
    down_c = lambda c: jnp.maximum(c - 1, 0)
    in_specs = [
        pl.BlockSpec((FFN_TM, d), lambda i, c: (i, 0)),
        pl.BlockSpec((halo, d), lambda i, c: (jnp.maximum(i * ratio - 1, 0), 0)),
        pl.BlockSpec((1, d), lambda i, c: (0, 0)),
        pl.BlockSpec((d, FFN_TC), lambda i, c: (0, up_c(c))),
        pl.BlockSpec((d, FFN_TC), lambda i, c: (0, nc + up_c(c))),
        pl.BlockSpec((taps, FFN_TC), lambda i, c: (0, up_c(c))),
        pl.BlockSpec((taps, FFN_TC), lambda i, c: (0, nc + up_c(c))),
        pl.BlockSpec((FFN_TC, d), lambda i, c: (down_c(c), 0)),
    ]
    args = [x2d, x2d, gain.reshape(1, d), w_up, w_up, conv_w, conv_w, w_down]
    if final:
        in_specs.append(pl.BlockSpec((1, d), lambda i, c: (0, 0)))
        args.append(final_gain.reshape(1, d))
    nbytes = (4 * FFN_TM * d * 4 + 2 * halo * d * 4 + (FFN_TM + halo) * d * 2 + 2 * 3 * d * FFN_TC * 2
              + 2 * FFN_TM * FFN_TC * 2 + 6 * (FFN_TM + halo) * FFN_TC * 4)
    return pl.pallas_call(
        functools.partial(_ffn_kernel, taps=taps, tiles_per_seq=seq // FFN_TM, final=final),
        grid=(n_rows // FFN_TM, nc + 1),
        in_specs=in_specs,
        out_specs=pl.BlockSpec((FFN_TM, d), lambda i, c: (i, 0)),
        out_shape=jax.ShapeDtypeStruct((n_rows, d), F32),
        scratch_shapes=[pltpu.VMEM((halo + FFN_TM, d), BF16), pltpu.VMEM((FFN_TM, FFN_TC), BF16),
                        pltpu.VMEM((FFN_TM, FFN_TC), BF16)],
        compiler_params=_params(("parallel", "arbitrary"), nbytes),
        name="ffn",
    )(*args)


ATT_T = 256
ATT_G = 4
LOG2_E = 1.4426950408889634
F32_EXP2_UNDERFLOW = -150.0


def _attn_kernel(q_ref, k_ref, v_ref, tri_ref, o_ref, acc_ref, carry_ref):
    blk = tri_ref.shape[0]
    n_g = acc_ref.shape[0]
    dh = acc_ref.shape[1]
    qi = pl.program_id(2)
    tri = tri_ref[...]

    acc_ref[...] = jnp.zeros(acc_ref.shape, F32)
    carry_ref[...] = jnp.zeros(carry_ref.shape, F32)

    def process(chains):
        def rows(kb):
            return pl.ds(pl.multiple_of(kb * blk, blk), blk)
        cols = lambda g: slice(g * dh, (g + 1) * dh)
        zs = [lax.dot_general(k_ref[rows(kb), cols(g)], q_ref[:, cols(g)], (((1,), (1,)), ((), ())),
                              preferred_element_type=F32) for g, kb, _ in chains]
        before = (lax.broadcasted_iota(jnp.int32, (blk, blk), 0) < lax.broadcasted_iota(jnp.int32, (blk, blk), 1))
        parts = []
        for (_, _, diagonal), z in zip(chains, zs):
            log_beta = jnp.minimum(z, 0.0) - jnp.log(1.0 + jnp.exp2(-jnp.abs(z))) * LOG2_E
            log_keep = log_beta - z
            if diagonal:
                log_keep = jnp.where(before, log_keep, 0.0)
            hi = log_keep.astype(BF16)
            lo = (log_keep - hi.astype(F32)).astype(BF16)
            parts.append((log_beta, log_keep, hi, lo))
        sufs = [_dot(tri, hi) + _dot(tri, lo) for _, _, hi, lo in parts]
        pvs = []
        for (g, kb, diagonal), (log_beta, _, _, _), suffix in zip(chains, parts, sufs):
            w = jnp.exp2(log_beta + suffix)
            if diagonal:
                w = jnp.where(before, w, 0.0)
            pvs.append(lax.dot_general(v_ref[rows(kb), cols(g)], w.astype(BF16), (((0,), (0,)), ((), ())),
                                       preferred_element_type=F32))
        for (g, _, _), (_, log_keep, _, _), pv in zip(chains, parts, pvs):
            carry = carry_ref[g]
            acc_ref[g] += jnp.exp2(carry) * pv
            carry_ref[g] = carry + jnp.sum(log_keep, axis=0, keepdims=True)

    diag_chains = [(g, qi, True) for g in range(n_g)]

    @pl.when(qi == 0)
    def _():
        process(diag_chains)

    @pl.when(qi > 0)
    def _():
        process(diag_chains + [(g, qi - 1, False) for g in range(n_g)])

    def cond(state):
        kb, max_carry = state
        return jnp.logical_and(kb >= 0, max_carry > F32_EXP2_UNDERFLOW)

    def body(state):
        kb, _ = state
        process([(g, kb, False) for g in range(n_g)])
        return kb - 1, jnp.max(carry_ref[...])

    lax.while_loop(cond, body, (qi - 2, jnp.max(carry_ref[...])))
    for g in range(n_g):
        o_ref[:, g * dh:(g + 1) * dh] = acc_ref[g].T.astype(o_ref.dtype)


def _attention(qkv, bsz, seq):
    d = qkv.shape[1] // 3
    dh = d // N_HEADS
    assert seq % ATT_T == 0 and N_HEADS % ATT_G == 0
    nq = seq // ATT_T
    hg = N_HEADS // ATT_G
    gw = ATT_G * dh
    idx = jnp.arange(ATT_T)
    tri = (idx[None, :] > idx[:, None]).astype(BF16)
    nbytes = (2 * ATT_T * gw * 2 + 2 * 2 * seq * gw * 2 + 2 * ATT_T * ATT_T * 2 + 2 * ATT_T * gw * 2
              + ATT_G * (dh + V7X_SUBLANES) * ATT_T * 4 + 2 * ATT_G * 10 * ATT_T * ATT_T * 4)
    return pl.pallas_call(
        _attn_kernel,
        grid=(bsz, hg, nq),
        in_specs=[
            pl.BlockSpec((ATT_T, gw), lambda b, h, i: (b * nq + i, h)),
            pl.BlockSpec((seq, gw), lambda b, h, i: (b, hg + h)),
            pl.BlockSpec((seq, gw), lambda b, h, i: (b, 2 * hg + h)),
            pl.BlockSpec((ATT_T, ATT_T), lambda b, h, i: (0, 0)),
        ],
        out_specs=pl.BlockSpec((ATT_T, gw), lambda b, h, i: (b * nq + i, h)),
        out_shape=jax.ShapeDtypeStruct((bsz * seq, d), BF16),
        scratch_shapes=[pltpu.VMEM((ATT_G, dh, ATT_T), F32), pltpu.VMEM((ATT_G, 1, ATT_T), F32)],
        compiler_params=_params(("parallel", "parallel", "arbitrary"), nbytes),
        name="attention",
    )(qkv, qkv, qkv, tri)


PROJ_TM = 1024
PROJ_TN = 1024


def kernel(x, ln_mix_even, w_in, conv_w, conv_b, conv_ln_g, conv_ln_b, ssm_lam_re, ssm_lam_im, ssm_log_step, ssm_b_re, ssm_b_im, ssm_c_re, ssm_c_im, ssm_d, ssm_w_glu, ssm_b_glu, w_out_even, ln_mix_odd, w_qkv, w_o, ln_ffn, ffn_w_up, ffn_conv_w, ffn_w_down, ln_final):
    bsz, seq, d = x.shape
    depth = ln_ffn.shape[0]
    assert depth == 2 and ln_mix_even.shape[0] == 1 and ln_mix_odd.shape[0] == 1
    tm = min(PROJ_TM, seq)
    tn = min(PROJ_TN, d)
    d_conv = conv_w.shape[-1]
    d_ssm = ssm_w_glu.shape[-1]
    d_in = w_in.shape[-1]
    assert d_conv == d_ssm and d_in == 2 * d_conv + d_ssm
    x2d = x.reshape(bsz * seq, d)

    u = _norm_matmul(x2d, ln_mix_even[0], w_in[0].astype(BF16), F32, tm=tm, tn=tn, name="in_proj")
    u3 = u.reshape(bsz, seq, d_in)
    ya = _conformer(u3, conv_w[0], conv_b[0], conv_ln_g[0], conv_ln_b[0])
    lbar_re, lbar_im, bb_re, bb_im = _zoh(ssm_lam_re[0], ssm_lam_im[0], ssm_log_step[0], ssm_b_re[0], ssm_b_im[0])
    yb = _s5(u3, 2, lbar_re, lbar_im, bb_re, bb_im, ssm_c_re[0], ssm_c_im[0], ssm_d[0],
             ssm_w_glu[0].astype(BF16), ssm_b_glu[0])
    x2d = _res_matmul([ya.reshape(bsz * seq, d_conv), yb.reshape(bsz * seq, d_ssm)], w_out_even[0].astype(BF16),
                      x2d, tm=tm, tn=tn, name="out_proj")
    x2d = _ffn(x2d, seq, ln_ffn[0], ffn_w_up[0].astype(BF16), ffn_conv_w[0], ffn_w_down[0].astype(BF16))

    dh = d // N_HEADS
    col_scale = jnp.concatenate([jnp.full((d,), dh ** -0.5 * LOG2_E, F32), jnp.ones((2 * d,), F32)])
    qkv = _norm_matmul(x2d, ln_mix_odd[0], w_qkv[0].astype(BF16), BF16, tm=tm, tn=tn, col_scale=col_scale,
                       name="qkv_proj")
    o = _attention(qkv, bsz, seq)
    x2d = _res_matmul([o], w_o[0].astype(BF16), x2d, tm=tm, tn=tn, name="o_proj")
    x2d = _ffn(x2d, seq, ln_ffn[1], ffn_w_up[1].astype(BF16), ffn_conv_w[1], ffn_w_down[1].astype(BF16),
               final_gain=ln_final)
    return x2d.reshape(bsz, seq, d)
```

```python
import functools

import jax
import jax.numpy as jnp
from jax import lax
from jax.experimental import pallas as pl
from jax.experimental.pallas import tpu as pltpu

F32 = jnp.float32
BF16 = jnp.bfloat16

EPS = 1e-6

V7X_SUBLANES = 8
V7X_LANES = 128
V7X_VMEM_BYTES = 64 * 1024 * 1024

N_HEADS = 16
N_SLABS = 8


def _vmem_limit(nbytes):
    return int(min(max(nbytes * 3 // 2, 16 * 1024 * 1024), V7X_VMEM_BYTES * 7 // 8))


def _params(semantics, nbytes):
    return pltpu.CompilerParams(dimension_semantics=semantics, vmem_limit_bytes=_vmem_limit(nbytes))


def _dot(a, b):
    return jnp.dot(a, b, preferred_element_type=F32)


def _rmsnorm(x, g):
    ms = jnp.mean(x * x, axis=-1, keepdims=True)
    return x * lax.rsqrt(ms + EPS) * g


def _sigmoid(x):
    return 1.0 / (1.0 + jnp.exp(-x))


def _to_time_major(src, dst_ref, row0):
    bsz, t_len, ch = src.shape
    for b in range(bsz):
        for s in range(ch // V7X_LANES):
            dst_ref[s, pl.ds(row0 + b, t_len, stride=bsz), :] = src[b, :, s * V7X_LANES:(s + 1) * V7X_LANES]


def _from_time_major(src_ref, o_ref):
    bsz, t_len, ch = o_ref.shape
    for b in range(bsz):
        for s in range(ch // V7X_LANES):
            o_ref[b, :, s * V7X_LANES:(s + 1) * V7X_LANES] = (
                src_ref[s, pl.ds(b, t_len, stride=bsz), :].astype(o_ref.dtype))


def _zoh_kernel(lr_ref, li_ref, ls_ref, br_ref, bi_ref, are_ref, aim_ref, bbr_ref, bbi_ref):
    lr = lr_ref[...]
    li = li_ref[...]
    step = jnp.exp(ls_ref[...])
    mag = jnp.exp(lr * step)
    lbar_re = mag * jnp.cos(li * step)
    lbar_im = mag * jnp.sin(li * step)
    num_re = lbar_re - 1.0
    num_im = lbar_im
    den = lr * lr + li * li
    f_re = (num_re * lr + num_im * li) / den
    f_im = (num_im * lr - num_re * li) / den
    br = br_ref[...]
    bi = bi_ref[...]
    are_ref[...] = lbar_re
    aim_ref[...] = lbar_im
    bbr_ref[...] = f_re * br - f_im * bi
    bbi_ref[...] = f_re * bi + f_im * br


def _zoh(lam_re, lam_im, log_step, b_re, b_im):
    g, p = lam_re.shape
    h = b_re.shape[-1]
    rep = lambda a: jnp.broadcast_to(a[:, None, :], (g, h, p)).reshape(g * h, p)
    args = (rep(lam_re), rep(lam_im), rep(jnp.broadcast_to(log_step[:, None], (g, p))),
            jnp.transpose(b_re, (0, 2, 1)).reshape(g * h, p),
            jnp.transpose(b_im, (0, 2, 1)).reshape(g * h, p))
    out = pl.pallas_call(
        _zoh_kernel,
        out_shape=[jax.ShapeDtypeStruct((g * h, p), F32)] * 4,
        name="zoh",
    )(*args)
    are, aim, bbr, bbi = [o.reshape(g, h, p) for o in out]
    return are[:, 0], aim[:, 0], bbr, bbi


def _norm_matmul_kernel(x_ref, g_ref, w_ref, *rest, has_scale):
    if has_scale:
        s_ref, o_ref, h_ref = rest
    else:
        o_ref, h_ref = rest

    @pl.when(pl.program_id(1) == 0)
    def _():
        h_ref[...] = _rmsnorm(x_ref[...], g_ref[...]).astype(BF16)

    acc = _dot(h_ref[...], w_ref[...])
    if has_scale:
        acc = acc * s_ref[...]
    o_ref[...] = acc.astype(o_ref.dtype)


def _norm_matmul(x2d, gain, w, out_dtype, *, tm, tn, col_scale=None, name):
    n_rows = x2d.shape[0]
    d, n = w.shape
    in_specs = [
        pl.BlockSpec((tm, d), lambda i, j: (i, 0)),
        pl.BlockSpec((1, d), lambda i, j: (0, 0)),
        pl.BlockSpec((d, tn), lambda i, j: (0, j)),
    ]
    args = [x2d, gain.reshape(1, d), w]
    if col_scale is not None:
        in_specs.append(pl.BlockSpec((1, tn), lambda i, j: (0, j)))
        args.append(col_scale.reshape(1, n))
    out_bytes = jnp.dtype(out_dtype).itemsize
    nbytes = 2 * tm * d * 4 + tm * d * 2 + 2 * d * tn * 2 + 2 * tm * tn * out_bytes + tm * tn * 4
    return pl.pallas_call(
        functools.partial(_norm_matmul_kernel, has_scale=col_scale is not None),
        grid=(n_rows // tm, n // tn),
        in_specs=in_specs,
        out_specs=pl.BlockSpec((tm, tn), lambda i, j: (i, j)),
        out_shape=jax.ShapeDtypeStruct((n_rows, n), out_dtype),
        scratch_shapes=[pltpu.VMEM((tm, d), BF16)],
        compiler_params=_params(("parallel", "arbitrary"), nbytes),
        name=name,
    )(*args)


def _res_matmul_kernel(*refs, n_a):
    a_refs = refs[:n_a]
    w_refs = refs[n_a:2 * n_a]
    res_ref, o_ref = refs[2 * n_a:]
    acc = res_ref[...]
    for a_ref, w_ref in zip(a_refs, w_refs):
        acc = acc + _dot(a_ref[...], w_ref[...])
    o_ref[...] = acc


def _res_matmul(a_list, w, res2d, *, tm, tn, name):
    n_a = len(a_list)
    n_rows, ka = a_list[0].shape
    assert n_a * ka == w.shape[0]
    n = w.shape[1]
    in_specs = [pl.BlockSpec((tm, ka), lambda i, j: (i, 0)) for _ in a_list]
    in_specs += [pl.BlockSpec((ka, tn), lambda i, j, r=r: (r, j)) for r in range(n_a)]
    in_specs += [pl.BlockSpec((tm, tn), lambda i, j: (i, j))]
    nbytes = 2 * tm * n_a * ka * 2 + 2 * n_a * ka * tn * 2 + 4 * tm * tn * 4 + tm * tn * 4
    return pl.pallas_call(
        functools.partial(_res_matmul_kernel, n_a=n_a),
        grid=(n_rows // tm, n // tn),
        in_specs=in_specs,
        out_specs=pl.BlockSpec((tm, tn), lambda i, j: (i, j)),
        out_shape=jax.ShapeDtypeStruct((n_rows, n), F32),
        compiler_params=_params(("parallel", "arbitrary"), nbytes),
        name=name,
    )(*a_list, *([w] * n_a), res2d)


CONV_T = 64
CONV_HALO = 32
CONV_TB = 16


def _conformer_kernel(val_ref, gate_ref, hval_ref, hgate_ref, wb_ref, cb_ref, lng_ref, lnb_ref, o_ref,
                      hx_ref, c_ref, *, taps):
    bsz, t_len, ch = val_ref.shape
    n_slab = ch // V7X_LANES
    keep = (pl.program_id(0) > 0).astype(F32)
    _to_time_major(hval_ref[...] * _sigmoid(hgate_ref[...]) * keep, hx_ref, 0)
    _to_time_major(val_ref[...] * _sigmoid(gate_ref[...]), hx_ref, CONV_HALO * bsz)

    lead = CONV_HALO - (taps - 1)
    for s in range(n_slab):
        def body(tb, carry, s=s):
            t0 = tb * CONV_TB
            acc = jnp.zeros((CONV_TB, bsz, V7X_LANES), F32)
            for k in range(taps):
                r0 = pl.multiple_of((t0 + lead + k) * bsz, bsz)
                rows = hx_ref[s, pl.ds(r0, CONV_TB * bsz), :].reshape(CONV_TB, bsz, V7X_LANES)
                acc = acc + wb_ref[k, :, s * V7X_LANES:(s + 1) * V7X_LANES][None] * rows
            c_ref[s, pl.ds(pl.multiple_of(t0 * bsz, bsz), CONV_TB * bsz), :] = acc.reshape(CONV_TB * bsz, V7X_LANES)
            return carry
        lax.fori_loop(0, t_len // CONV_TB, body, 0)

    h = jnp.concatenate([c_ref[s] for s in range(n_slab)], axis=-1) + cb_ref[...]
    mu = jnp.mean(h, axis=-1, keepdims=True)
    dlt = h - mu
    var = jnp.mean(dlt * dlt, axis=-1, keepdims=True)
    y = dlt * lax.rsqrt(var + EPS) * lng_ref[...] + lnb_ref[...]
    y = y * _sigmoid(y)
    for s in range(n_slab):
        c_ref[s] = y[:, s * V7X_LANES:(s + 1) * V7X_LANES]
    _from_time_major(c_ref, o_ref)


def _conformer(u3, conv_w, conv_b, ln_g, ln_b):
    bsz, seq, _ = u3.shape
    taps, ch = conv_w.shape
    assert taps - 1 <= CONV_HALO and CONV_T % CONV_HALO == 0 and seq % CONV_T == 0 and ch % V7X_LANES == 0
    wb = jnp.broadcast_to(conv_w[:, None, :], (taps, bsz, ch))
    ratio = CONV_T // CONV_HALO
    halo_idx = lambda c: (lambda t: (0, jnp.maximum(t * ratio - 1, 0), c))
    row = lambda a: a.reshape(1, ch)
    full2 = lambda t: (0, 0)
    nbytes = (2 * 2 * (CONV_T + CONV_HALO) * bsz * ch * 4 + 2 * taps * bsz * ch * 4
              + (2 * CONV_T + CONV_HALO) * bsz * ch * 4 + 2 * CONV_T * bsz * ch * 2 + 6 * CONV_T * bsz * ch * 4)
    return pl.pallas_call(
        functools.partial(_conformer_kernel, taps=taps),
        grid=(seq // CONV_T,),
        in_specs=[
            pl.BlockSpec((bsz, CONV_T, ch), lambda t: (0, t, 0)),
            pl.BlockSpec((bsz, CONV_T, ch), lambda t: (0, t, 1)),
            pl.BlockSpec((bsz, CONV_HALO, ch), halo_idx(0)),
            pl.BlockSpec((bsz, CONV_HALO, ch), halo_idx(1)),
            pl.BlockSpec((taps, bsz, ch), lambda t: (0, 0, 0)),
            pl.BlockSpec((1, ch), full2),
            pl.BlockSpec((1, ch), full2),
            pl.BlockSpec((1, ch), full2),
        ],
        out_specs=pl.BlockSpec((bsz, CONV_T, ch), lambda t: (0, t, 0)),
        out_shape=jax.ShapeDtypeStruct((bsz, seq, ch), BF16),
        scratch_shapes=[pltpu.VMEM((ch // V7X_LANES, (CONV_HALO + CONV_T) * bsz, V7X_LANES), F32),
                        pltpu.VMEM((ch // V7X_LANES, CONV_T * bsz, V7X_LANES), F32)],
        compiler_params=_params(("parallel",), nbytes),
        name="conformer",
    )(u3, u3, u3, u3, wb, row(conv_b), row(ln_g), row(ln_b))


S5_T = 32


def _s5_kernel(u_ref, wb_ref, lbar_ref, wcr_ref, wci_ref, d_ref, wg_ref, bg_ref, o_ref, ut_ref, bu_ref, st_ref):
    bsz, t_len, ch = u_ref.shape
    rows = t_len * bsz
    n_slabs, _, width = lbar_ref.shape
    half = width // 2
    cin = wb_ref.shape[1]

    @pl.when(pl.program_id(0) == 0)
    def _():
        st_ref[...] = jnp.zeros(st_ref.shape, F32)

    _to_time_major(u_ref[...], ut_ref, 0)
    u = jnp.concatenate([ut_ref[s] for s in range(ch // V7X_LANES)], axis=-1)
    ub = u.astype(BF16)
    for j in range(n_slabs):
        bu_ref[j] = _dot(ub[:, j * cin:(j + 1) * cin], wb_ref[j])

    def scan_slab(j, carry):
        a = lbar_ref[j]
        ar, ai = a[:, :half], a[:, half:]
        s = st_ref[j]
        xr, xi = s[:, :half], s[:, half:]
        for t in range(t_len):
            b = bu_ref[j, t * bsz:(t + 1) * bsz, :]
            xr, xi = ar * xr - ai * xi + b[:, :half], ar * xi + ai * xr + b[:, half:]
            bu_ref[j, t * bsz:(t + 1) * bsz, :] = jnp.concatenate([xr, xi], axis=-1)
        st_ref[j] = jnp.concatenate([xr, xi], axis=-1)
        return carry
    lax.fori_loop(0, n_slabs, scan_slab, 0)

    ys = []
    for j in range(n_slabs):
        s = bu_ref[j]
        ys.append(_dot(s[:, :half].astype(BF16), wcr_ref[j]) - _dot(s[:, half:].astype(BF16), wci_ref[j]))
    y = jnp.concatenate(ys, axis=-1) + d_ref[...] * u
    y = jax.nn.gelu(y)
    gate = _sigmoid(_dot(y.astype(BF16), wg_ref[...]) + bg_ref[...])
    y = y * gate
    for s in range(ch // V7X_LANES):
        ut_ref[s] = y[:, s * V7X_LANES:(s + 1) * V7X_LANES]
    _from_time_major(ut_ref, o_ref)


def _s5(u3, cblk, lbar_re, lbar_im, bb_re, bb_im, c_re, c_im, d_skip, w_glu, b_glu):
    bsz, seq, _ = u3.shape
    g, h, p = bb_re.shape
    ch = g * h
    gs = g // N_SLABS
    half = gs * p
    eye = jnp.eye(gs, dtype=F32)

    def place_b(bb):
        return jnp.einsum("jghp,gk->jghkp", bb.reshape(N_SLABS, gs, h, p), eye).reshape(N_SLABS, gs * h, half)

    def place_c(c):
        return jnp.einsum("jghp,gk->jgpkh", c.reshape(N_SLABS, gs, h, p), eye).reshape(N_SLABS, half, gs * h)

    wb = jnp.concatenate([place_b(bb_re), place_b(bb_im)], axis=-1).astype(BF16)
    wcr = place_c(c_re).astype(BF16)
    wci = place_c(c_im).astype(BF16)
    lbar = jnp.concatenate([lbar_re.reshape(N_SLABS, half), lbar_im.reshape(N_SLABS, half)], axis=-1)
    lbar = jnp.broadcast_to(lbar[:, None, :], (N_SLABS, bsz, 2 * half))
    rows = S5_T * bsz
    full2 = lambda t: (0, 0)
    full3 = lambda t: (0, 0, 0)
    nbytes = (2 * rows * ch * 4 + 2 * rows * ch * 2 + rows * ch * 4 + N_SLABS * rows * 2 * half * 4
              + 2 * (wb.size + wcr.size + wci.size + w_glu.size) * 2 + 2 * lbar.size * 4 + 6 * rows * ch * 4)
    return pl.pallas_call(
        _s5_kernel,
        grid=(seq // S5_T,),
        in_specs=[
            pl.BlockSpec((bsz, S5_T, ch), lambda t: (0, t, cblk)),
            pl.BlockSpec(wb.shape, full3),
            pl.BlockSpec(lbar.shape, full3),
            pl.BlockSpec(wcr.shape, full3),
            pl.BlockSpec(wci.shape, full3),
            pl.BlockSpec((1, ch), full2),
            pl.BlockSpec(w_glu.shape, full2),
            pl.BlockSpec((1, ch), full2),
        ],
        out_specs=pl.BlockSpec((bsz, S5_T, ch), lambda t: (0, t, 0)),
        out_shape=jax.ShapeDtypeStruct((bsz, seq, ch), BF16),
        scratch_shapes=[pltpu.VMEM((ch // V7X_LANES, rows, V7X_LANES), F32),
                        pltpu.VMEM((N_SLABS, rows, 2 * half), F32),
                        pltpu.VMEM((N_SLABS, bsz, 2 * half), F32)],
        compiler_params=_params(("arbitrary",), nbytes),
        name="s5",
    )(u3, wb, lbar, wcr, wci, d_skip.reshape(1, ch), w_glu, b_glu.reshape(1, ch))


FFN_T = 64
FFN_TC = 512


def _ffn_kernel(x_ref, g_ref, wg_ref, wv_ref, cg_ref, cv_ref, wd_ref, *rest, taps, final):
    if final:
        fg_ref, o_ref, acc_ref, h_ref, carry_ref = rest
    else:
        o_ref, acc_ref, h_ref, carry_ref = rest
    bsz, t_len, d = x_ref.shape
    tm = bsz * t_len
    n_slab = d // V7X_LANES
    hist = (taps - 1) * bsz
    c = pl.program_id(1)

    def full_rows(ref):
        return jnp.concatenate([ref[s] for s in range(n_slab)], axis=-1)

    @pl.when(jnp.logical_and(pl.program_id(0) == 0, c == 0))
    def _():
        carry_ref[...] = jnp.zeros(carry_ref.shape, F32)

    @pl.when(c == 0)
    def _():
        _to_time_major(x_ref[...], acc_ref, 0)
        h_ref[...] = _rmsnorm(full_rows(acc_ref), g_ref[...]).astype(BF16)

    h = h_ref[...]

    def conv(u, cw, slot):
        ext = jnp.concatenate([carry_ref[c, slot], u], axis=0)
        carry_ref[c, slot] = u[tm - hist:tm]
        out = cw[0:1, :] * ext[0:tm]
        for k in range(1, taps):
            out = out + cw[k:k + 1, :] * ext[k * bsz:k * bsz + tm]
        return out

    gate = conv(_dot(h, wg_ref[...]), cg_ref[...], 0)
    val = conv(_dot(h, wv_ref[...]), cv_ref[...], 1)
    act = (gate * _sigmoid(gate) * val).astype(BF16)
    part = _dot(act, wd_ref[...])
    for s in range(n_slab):
        acc_ref[s] += part[:, s * V7X_LANES:(s + 1) * V7X_LANES]

    @pl.when(c == pl.num_programs(1) - 1)
    def _():
        if final:
            y = _rmsnorm(full_rows(acc_ref), fg_ref[...])
            for s in range(n_slab):
                acc_ref[s] = y[:, s * V7X_LANES:(s + 1) * V7X_LANES]
        _from_time_major(acc_ref, o_ref)


def _ffn(x3, gain, w_up, conv_w, w_down, final_gain=None):
    bsz, seq, d = x3.shape
    taps = conv_w.shape[0]
    d_ff = w_down.shape[0]
    assert (taps - 1 <= FFN_T and seq % FFN_T == 0 and d_ff % FFN_TC == 0
            and bsz % V7X_SUBLANES == 0 and d % V7X_LANES == 0)
    nc = d_ff // FFN_TC
    tm = FFN_T * bsz
    hist = (taps - 1) * bsz
    final = final_gain is not None
    in_specs = [
        pl.BlockSpec((bsz, FFN_T, d), lambda i, c: (0, i, 0)),
        pl.BlockSpec((1, d), lambda i, c: (0, 0)),
        pl.BlockSpec((d, FFN_TC), lambda i, c: (0, c)),
        pl.BlockSpec((d, FFN_TC), lambda i, c: (0, nc + c)),
        pl.BlockSpec((taps, FFN_TC), lambda i, c: (0, c)),
        pl.BlockSpec((taps, FFN_TC), lambda i, c: (0, nc + c)),
        pl.BlockSpec((FFN_TC, d), lambda i, c: (c, 0)),
    ]
    args = [x3, gain.reshape(1, d), w_up, w_up, conv_w, conv_w, w_down]
    if final:
        in_specs.append(pl.BlockSpec((1, d), lambda i, c: (0, 0)))
        args.append(final_gain.reshape(1, d))
    nbytes = (5 * tm * d * 4 + tm * d * 2 + 2 * 3 * d * FFN_TC * 2 + nc * 2 * hist * FFN_TC * 4
              + 6 * (tm + hist) * FFN_TC * 4)
    return pl.pallas_call(
        functools.partial(_ffn_kernel, taps=taps, final=final),
        grid=(seq // FFN_T, nc),
        in_specs=in_specs,
        out_specs=pl.BlockSpec((bsz, FFN_T, d), lambda i, c: (0, i, 0)),
        out_shape=jax.ShapeDtypeStruct((bsz, seq, d), F32),
        scratch_shapes=[pltpu.VMEM((d // V7X_LANES, tm, V7X_LANES), F32),
                        pltpu.VMEM((tm, d), BF16),
                        pltpu.VMEM((nc, 2, hist, FFN_TC), F32)],
        compiler_params=_params(("arbitrary", "arbitrary"), nbytes),
        name="ffn",
    )(*args)


ATT_T = 256
ATT_G = 4
LOG2_E = 1.4426950408889634
F32_EXP2_UNDERFLOW = -150.0


def _attn_kernel(q_ref, k_ref, v_ref, tri_ref, o_ref, acc_ref, carry_ref):
    blk = tri_ref.shape[0]
    n_g = acc_ref.shape[0]
    dh = acc_ref.shape[1]
    qi = pl.program_id(2)
    tri = tri_ref[...]

    acc_ref[...] = jnp.zeros(acc_ref.shape, F32)
    carry_ref[...] = jnp.zeros(carry_ref.shape, F32)

    def process(chains):
        def rows(kb):
            return pl.ds(pl.multiple_of(kb * blk, blk), blk)
        cols = lambda g: slice(g * dh, (g + 1) * dh)
        zs = [lax.dot_general(k_ref[rows(kb), cols(g)], q_ref[:, cols(g)], (((1,), (1,)), ((), ())),
                              preferred_element_type=F32) for g, kb, _ in chains]
        before = (lax.broadcasted_iota(jnp.int32, (blk, blk), 0) < lax.broadcasted_iota(jnp.int32, (blk, blk), 1))
        parts = []
        for (_, _, diagonal), z in zip(chains, zs):
            log_beta = jnp.minimum(z, 0.0) - jnp.log(1.0 + jnp.exp2(-jnp.abs(z))) * LOG2_E
            log_keep = log_beta - z
            if diagonal:
                log_keep = jnp.where(before, log_keep, 0.0)
            hi = log_keep.astype(BF16)
            lo = (log_keep - hi.astype(F32)).astype(BF16)
            parts.append((log_beta, log_keep, hi, lo))
        sufs = [_dot(tri, hi) + _dot(tri, lo) for _, _, hi, lo in parts]
        pvs = []
        for (g, kb, diagonal), (log_beta, _, _, _), suffix in zip(chains, parts, sufs):
            w = jnp.exp2(log_beta + suffix)
            if diagonal:
                w = jnp.where(before, w, 0.0)
            pvs.append(lax.dot_general(v_ref[rows(kb), cols(g)], w.astype(BF16), (((0,), (0,)), ((), ())),
                                       preferred_element_type=F32))
        for (g, _, _), (_, log_keep, _, _), pv in zip(chains, parts, pvs):
            carry = carry_ref[g]
            acc_ref[g] += jnp.exp2(carry) * pv
            carry_ref[g] = carry + jnp.sum(log_keep, axis=0, keepdims=True)

    diag_chains = [(g, qi, True) for g in range(n_g)]

    @pl.when(qi == 0)
    def _():
        process(diag_chains)

    @pl.when(qi > 0)
    def _():
        process(diag_chains + [(g, qi - 1, False) for g in range(n_g)])

    def cond(state):
        kb, max_carry = state
        return jnp.logical_and(kb >= 0, max_carry > F32_EXP2_UNDERFLOW)

    def body(state):
        kb, _ = state
        process([(g, kb, False) for g in range(n_g)])
        return kb - 1, jnp.max(carry_ref[...])

    lax.while_loop(cond, body, (qi - 2, jnp.max(carry_ref[...])))
    for g in range(n_g):
        o_ref[:, g * dh:(g + 1) * dh] = acc_ref[g].T.astype(o_ref.dtype)


def _attention(qkv, bsz, seq):
    d = qkv.shape[1] // 3
    dh = d // N_HEADS
    assert seq % ATT_T == 0 and N_HEADS % ATT_G == 0
    nq = seq // ATT_T
    hg = N_HEADS // ATT_G
    gw = ATT_G * dh
    idx = jnp.arange(ATT_T)
    tri = (idx[None, :] > idx[:, None]).astype(BF16)
    nbytes = (2 * ATT_T * gw * 2 + 2 * 2 * seq * gw * 2 + 2 * ATT_T * ATT_T * 2 + 2 * ATT_T * gw * 2
              + ATT_G * (dh + V7X_SUBLANES) * ATT_T * 4 + 2 * ATT_G * 10 * ATT_T * ATT_T * 4)
    return pl.pallas_call(
        _attn_kernel,
        grid=(bsz, hg, nq),
        in_specs=[
            pl.BlockSpec((ATT_T, gw), lambda b, h, i: (b * nq + i, h)),
            pl.BlockSpec((seq, gw), lambda b, h, i: (b, hg + h)),
            pl.BlockSpec((seq, gw), lambda b, h, i: (b, 2 * hg + h)),
            pl.BlockSpec((ATT_T, ATT_T), lambda b, h, i: (0, 0)),
        ],
        out_specs=pl.BlockSpec((ATT_T, gw), lambda b, h, i: (b * nq + i, h)),
        out_shape=jax.ShapeDtypeStruct((bsz * seq, d), BF16),
        scratch_shapes=[pltpu.VMEM((ATT_G, dh, ATT_T), F32), pltpu.VMEM((ATT_G, 1, ATT_T), F32)],
        compiler_params=_params(("parallel", "parallel", "arbitrary"), nbytes),
        name="attention",
    )(qkv, qkv, qkv, tri)


PROJ_TM = 1024
PROJ_TN = 1024


def kernel(x, ln_mix_even, w_in, conv_w, conv_b, conv_ln_g, conv_ln_b, ssm_lam_re, ssm_lam_im, ssm_log_step, ssm_b_re, ssm_b_im, ssm_c_re, ssm_c_im, ssm_d, ssm_w_glu, ssm_b_glu, w_out_even, ln_mix_odd, w_qkv, w_o, ln_ffn, ffn_w_up, ffn_conv_w, ffn_w_down, ln_final):
    bsz, seq, d = x.shape
    depth = ln_ffn.shape[0]
    assert depth == 2 and ln_mix_even.shape[0] == 1 and ln_mix_odd.shape[0] == 1
    tm = min(PROJ_TM, seq)
    tn = min(PROJ_TN, d)
    d_conv = conv_w.shape[-1]
    d_ssm = ssm_w_glu.shape[-1]
    d_in = w_in.shape[-1]
    assert d_conv == d_ssm and d_in == 2 * d_conv + d_ssm
    x2d = x.reshape(bsz * seq, d)

    u = _norm_matmul(x2d, ln_mix_even[0], w_in[0].astype(BF16), F32, tm=tm, tn=tn, name="in_proj")
    u3 = u.reshape(bsz, seq, d_in)
    ya = _conformer(u3, conv_w[0], conv_b[0], conv_ln_g[0], conv_ln_b[0])
    lbar_re, lbar_im, bb_re, bb_im = _zoh(ssm_lam_re[0], ssm_lam_im[0], ssm_log_step[0], ssm_b_re[0], ssm_b_im[0])
    yb = _s5(u3, 2, lbar_re, lbar_im, bb_re, bb_im, ssm_c_re[0], ssm_c_im[0], ssm_d[0],
             ssm_w_glu[0].astype(BF16), ssm_b_glu[0])
    x2d = _res_matmul([ya.reshape(bsz * seq, d_conv), yb.reshape(bsz * seq, d_ssm)], w_out_even[0].astype(BF16),
                      x2d, tm=tm, tn=tn, name="out_proj")
    x2d = _ffn(x2d.reshape(bsz, seq, d), ln_ffn[0], ffn_w_up[0].astype(BF16), ffn_conv_w[0],
               ffn_w_down[0].astype(BF16)).reshape(bsz * seq, d)

    dh = d // N_HEADS
    col_scale = jnp.concatenate([jnp.full((d,), dh ** -0.5 * LOG2_E, F32), jnp.ones((2 * d,), F32)])
    qkv = _norm_matmul(x2d, ln_mix_odd[0], w_qkv[0].astype(BF16), BF16, tm=tm, tn=tn, col_scale=col_scale,
                       name="qkv_proj")
    o = _attention(qkv, bsz, seq)
    x2d = _res_matmul([o], w_o[0].astype(BF16), x2d, tm=tm, tn=tn, name="o_proj")
    return _ffn(x2d.reshape(bsz, seq, d), ln_ffn[1], ffn_w_up[1].astype(BF16), ffn_conv_w[1],
                ffn_w_down[1].astype(BF16), final_gain=ln_final)
```

```python
import functools

import jax
import jax.numpy as jnp
from jax import lax
from jax.experimental import pallas as pl
from jax.experimental.pallas import tpu as pltpu

F32 = jnp.float32
BF16 = jnp.bfloat16

EPS = 1e-6

V7X_SUBLANES = 8
V7X_LANES = 128
V7X_VMEM_BYTES = 64 * 1024 * 1024

N_HEADS = 16
N_SLABS = 8


def _vmem_limit(nbytes):
    return int(min(max(nbytes * 3 // 2, 16 * 1024 * 1024), V7X_VMEM_BYTES * 7 // 8))


def _params(semantics, nbytes):
    return pltpu.CompilerParams(dimension_semantics=semantics, vmem_limit_bytes=_vmem_limit(nbytes))


def _dot(a, b):
    return jnp.dot(a, b, preferred_element_type=F32)


def _rmsnorm(x, g):
    ms = jnp.mean(x * x, axis=-1, keepdims=True)
    return x * lax.rsqrt(ms + EPS) * g


def _sigmoid(x):
    return 1.0 / (1.0 + jnp.exp(-x))


def _to_time_major(src, dst_ref, row0):
    bsz, t_len, ch = src.shape
    for b in range(bsz):
        for s in range(ch // V7X_LANES):
            dst_ref[s, pl.ds(row0 + b, t_len, stride=bsz), :] = src[b, :, s * V7X_LANES:(s + 1) * V7X_LANES]


def _from_time_major(src_ref, o_ref):
    bsz, t_len, ch = o_ref.shape
    for b in range(bsz):
        for s in range(ch // V7X_LANES):
            o_ref[b, :, s * V7X_LANES:(s + 1) * V7X_LANES] = (
                src_ref[s, pl.ds(b, t_len, stride=bsz), :].astype(o_ref.dtype))


def _zoh_kernel(lr_ref, li_ref, ls_ref, br_ref, bi_ref, are_ref, aim_ref, bbr_ref, bbi_ref):
    lr = lr_ref[...]
    li = li_ref[...]
    step = jnp.exp(ls_ref[...])
    mag = jnp.exp(lr * step)
    lbar_re = mag * jnp.cos(li * step)
    lbar_im = mag * jnp.sin(li * step)
    num_re = lbar_re - 1.0
    num_im = lbar_im
    den = lr * lr + li * li
    f_re = (num_re * lr + num_im * li) / den
    f_im = (num_im * lr - num_re * li) / den
    br = br_ref[...]
    bi = bi_ref[...]
    are_ref[...] = lbar_re
    aim_ref[...] = lbar_im
    bbr_ref[...] = f_re * br - f_im * bi
    bbi_ref[...] = f_re * bi + f_im * br


def _zoh(lam_re, lam_im, log_step, b_re, b_im):
    g, p = lam_re.shape
    h = b_re.shape[-1]
    rep = lambda a: jnp.broadcast_to(a[:, None, :], (g, h, p)).reshape(g * h, p)
    args = (rep(lam_re), rep(lam_im), rep(jnp.broadcast_to(log_step[:, None], (g, p))),
            jnp.transpose(b_re, (0, 2, 1)).reshape(g * h, p),
            jnp.transpose(b_im, (0, 2, 1)).reshape(g * h, p))
    out = pl.pallas_call(
        _zoh_kernel,
        out_shape=[jax.ShapeDtypeStruct((g * h, p), F32)] * 4,
        name="zoh",
    )(*args)
    are, aim, bbr, bbi = [o.reshape(g, h, p) for o in out]
    return are[:, 0], aim[:, 0], bbr, bbi


def _norm_matmul_kernel(x_ref, g_ref, w_ref, *rest, has_scale):
    if has_scale:
        s_ref, o_ref, h_ref = rest
    else:
        o_ref, h_ref = rest

    @pl.when(pl.program_id(1) == 0)
    def _():
        h_ref[...] = _rmsnorm(x_ref[...], g_ref[...]).astype(BF16)

    acc = _dot(h_ref[...], w_ref[...])
    if has_scale:
        acc = acc * s_ref[...]
    o_ref[...] = acc.astype(o_ref.dtype)


def _norm_matmul(x2d, gain, w, out_dtype, *, tm, tn, col_scale=None, name):
    n_rows = x2d.shape[0]
    d, n = w.shape
    in_specs = [
        pl.BlockSpec((tm, d), lambda i, j: (i, 0)),
        pl.BlockSpec((1, d), lambda i, j: (0, 0)),
        pl.BlockSpec((d, tn), lambda i, j: (0, j)),
    ]
    args = [x2d, gain.reshape(1, d), w]
    if col_scale is not None:
        in_specs.append(pl.BlockSpec((1, tn), lambda i, j: (0, j)))
        args.append(col_scale.reshape(1, n))
    out_bytes = jnp.dtype(out_dtype).itemsize
    nbytes = 2 * tm * d * 4 + tm * d * 2 + 2 * d * tn * 2 + 2 * tm * tn * out_bytes + tm * tn * 4
    return pl.pallas_call(
        functools.partial(_norm_matmul_kernel, has_scale=col_scale is not None),
        grid=(n_rows // tm, n // tn),
        in_specs=in_specs,
        out_specs=pl.BlockSpec((tm, tn), lambda i, j: (i, j)),
        out_shape=jax.ShapeDtypeStruct((n_rows, n), out_dtype),
        scratch_shapes=[pltpu.VMEM((tm, d), BF16)],
        compiler_params=_params(("parallel", "arbitrary"), nbytes),
        name=name,
    )(*args)


def _res_matmul_kernel(*refs, n_a):
    a_refs = refs[:n_a]
    w_refs = refs[n_a:2 * n_a]
    res_ref, o_ref = refs[2 * n_a:]
    acc = res_ref[...]
    for a_ref, w_ref in zip(a_refs, w_refs):
        acc = acc + _dot(a_ref[...], w_ref[...])
    o_ref[...] = acc


def _res_matmul(a_list, w, res2d, *, tm, tn, name):
    n_a = len(a_list)
    n_rows, ka = a_list[0].shape
    assert n_a * ka == w.shape[0]
    n = w.shape[1]
    in_specs = [pl.BlockSpec((tm, ka), lambda i, j: (i, 0)) for _ in a_list]
    in_specs += [pl.BlockSpec((ka, tn), lambda i, j, r=r: (r, j)) for r in range(n_a)]
    in_specs += [pl.BlockSpec((tm, tn), lambda i, j: (i, j))]
    nbytes = 2 * tm * n_a * ka * 2 + 2 * n_a * ka * tn * 2 + 4 * tm * tn * 4 + tm * tn * 4
    return pl.pallas_call(
        functools.partial(_res_matmul_kernel, n_a=n_a),
        grid=(n_rows // tm, n // tn),
        in_specs=in_specs,
        out_specs=pl.BlockSpec((tm, tn), lambda i, j: (i, j)),
        out_shape=jax.ShapeDtypeStruct((n_rows, n), F32),
        compiler_params=_params(("parallel", "arbitrary"), nbytes),
        name=name,
    )(*a_list, *([w] * n_a), res2d)


CONV_T = 64
CONV_HALO = 32
CONV_TB = 16


def _conformer_kernel(val_ref, gate_ref, hval_ref, hgate_ref, wb_ref, cb_ref, lng_ref, lnb_ref, o_ref,
                      hx_ref, c_ref, *, taps):
    bsz, t_len, ch = val_ref.shape
    n_slab = ch // V7X_LANES
    keep = (pl.program_id(0) > 0).astype(F32)
    _to_time_major(hval_ref[...] * _sigmoid(hgate_ref[...]) * keep, hx_ref, 0)
    _to_time_major(val_ref[...] * _sigmoid(gate_ref[...]), hx_ref, CONV_HALO * bsz)

    lead = CONV_HALO - (taps - 1)
    for s in range(n_slab):
        def body(tb, carry, s=s):
            t0 = tb * CONV_TB
            acc = jnp.zeros((CONV_TB, bsz, V7X_LANES), F32)
            for k in range(taps):
                r0 = pl.multiple_of((t0 + lead + k) * bsz, bsz)
                rows = hx_ref[s, pl.ds(r0, CONV_TB * bsz), :].reshape(CONV_TB, bsz, V7X_LANES)
                acc = acc + wb_ref[k, :, s * V7X_LANES:(s + 1) * V7X_LANES][None] * rows
            c_ref[s, pl.ds(pl.multiple_of(t0 * bsz, bsz), CONV_TB * bsz), :] = acc.reshape(CONV_TB * bsz, V7X_LANES)
            return carry
        lax.fori_loop(0, t_len // CONV_TB, body, 0)

    h = jnp.concatenate([c_ref[s] for s in range(n_slab)], axis=-1) + cb_ref[...]
    mu = jnp.mean(h, axis=-1, keepdims=True)
    dlt = h - mu
    var = jnp.mean(dlt * dlt, axis=-1, keepdims=True)
    y = dlt * lax.rsqrt(var + EPS) * lng_ref[...] + lnb_ref[...]
    y = y * _sigmoid(y)
    for s in range(n_slab):
        c_ref[s] = y[:, s * V7X_LANES:(s + 1) * V7X_LANES]
    _from_time_major(c_ref, o_ref)


def _conformer(u3, conv_w, conv_b, ln_g, ln_b):
    bsz, seq, _ = u3.shape
    taps, ch = conv_w.shape
    assert taps - 1 <= CONV_HALO and CONV_T % CONV_HALO == 0 and seq % CONV_T == 0 and ch % V7X_LANES == 0
    wb = jnp.broadcast_to(conv_w[:, None, :], (taps, bsz, ch))
    ratio = CONV_T // CONV_HALO
    halo_idx = lambda c: (lambda t: (0, jnp.maximum(t * ratio - 1, 0), c))
    row = lambda a: a.reshape(1, ch)
    full2 = lambda t: (0, 0)
    nbytes = (2 * 2 * (CONV_T + CONV_HALO) * bsz * ch * 4 + 2 * taps * bsz * ch * 4
              + (2 * CONV_T + CONV_HALO) * bsz * ch * 4 + 2 * CONV_T * bsz * ch * 2 + 6 * CONV_T * bsz * ch * 4)
    return pl.pallas_call(
        functools.partial(_conformer_kernel, taps=taps),
        grid=(seq // CONV_T,),
        in_specs=[
            pl.BlockSpec((bsz, CONV_T, ch), lambda t: (0, t, 0)),
            pl.BlockSpec((bsz, CONV_T, ch), lambda t: (0, t, 1)),
            pl.BlockSpec((bsz, CONV_HALO, ch), halo_idx(0)),
            pl.BlockSpec((bsz, CONV_HALO, ch), halo_idx(1)),
            pl.BlockSpec((taps, bsz, ch), lambda t: (0, 0, 0)),
            pl.BlockSpec((1, ch), full2),
            pl.BlockSpec((1, ch), full2),
            pl.BlockSpec((1, ch), full2),
        ],
        out_specs=pl.BlockSpec((bsz, CONV_T, ch), lambda t: (0, t, 0)),
        out_shape=jax.ShapeDtypeStruct((bsz, seq, ch), BF16),
        scratch_shapes=[pltpu.VMEM((ch // V7X_LANES, (CONV_HALO + CONV_T) * bsz, V7X_LANES), F32),
                        pltpu.VMEM((ch // V7X_LANES, CONV_T * bsz, V7X_LANES), F32)],
        compiler_params=_params(("parallel",), nbytes),
        name="conformer",
    )(u3, u3, u3, u3, wb, row(conv_b), row(ln_g), row(ln_b))


S5_T = 32


def _s5_kernel(u_ref, wb_ref, lbar_ref, wcr_ref, wci_ref, d_ref, wg_ref, bg_ref, o_ref, ut_ref, bu_ref, st_ref):
    bsz, t_len, ch = u_ref.shape
    rows = t_len * bsz
    n_slabs, _, width = lbar_ref.shape
    half = width // 2
    cin = wb_ref.shape[1]

    @pl.when(pl.program_id(0) == 0)
    def _():
        st_ref[...] = jnp.zeros(st_ref.shape, F32)

    _to_time_major(u_ref[...], ut_ref, 0)
    u = jnp.concatenate([ut_ref[s] for s in range(ch // V7X_LANES)], axis=-1)
    ub = u.astype(BF16)
    for j in range(n_slabs):
        bu_ref[j] = _dot(ub[:, j * cin:(j + 1) * cin], wb_ref[j])

    ys = []
    for j in range(n_slabs):
        a = lbar_ref[j]
        ar, ai = a[:, :half], a[:, half:]
        s = st_ref[j]
        xr, xi = s[:, :half], s[:, half:]
        for t in range(t_len):
            b = bu_ref[j, t * bsz:(t + 1) * bsz, :]
            xr, xi = ar * xr - ai * xi + b[:, :half], ar * xi + ai * xr + b[:, half:]
            bu_ref[j, t * bsz:(t + 1) * bsz, :] = jnp.concatenate([xr, xi], axis=-1)
        st_ref[j] = jnp.concatenate([xr, xi], axis=-1)
        s = bu_ref[j]
        ys.append(_dot(s[:, :half].astype(BF16), wcr_ref[j]) - _dot(s[:, half:].astype(BF16), wci_ref[j]))
    y = jnp.concatenate(ys, axis=-1) + d_ref[...] * u
    y = jax.nn.gelu(y)
    gate = _sigmoid(_dot(y.astype(BF16), wg_ref[...]) + bg_ref[...])
    y = y * gate
    for s in range(ch // V7X_LANES):
        ut_ref[s] = y[:, s * V7X_LANES:(s + 1) * V7X_LANES]
    _from_time_major(ut_ref, o_ref)


def _s5(u3, cblk, lbar_re, lbar_im, bb_re, bb_im, c_re, c_im, d_skip, w_glu, b_glu):
    bsz, seq, _ = u3.shape
    g, h, p = bb_re.shape
    ch = g * h
    gs = g // N_SLABS
    half = gs * p
    eye = jnp.eye(gs, dtype=F32)

    def place_b(bb):
        return jnp.einsum("jghp,gk->jghkp", bb.reshape(N_SLABS, gs, h, p), eye).reshape(N_SLABS, gs * h, half)

    def place_c(c):
        return jnp.einsum("jghp,gk->jgpkh", c.reshape(N_SLABS, gs, h, p), eye).reshape(N_SLABS, half, gs * h)

    wb = jnp.concatenate([place_b(bb_re), place_b(bb_im)], axis=-1).astype(BF16)
    wcr = place_c(c_re).astype(BF16)
    wci = place_c(c_im).astype(BF16)
    lbar = jnp.concatenate([lbar_re.reshape(N_SLABS, half), lbar_im.reshape(N_SLABS, half)], axis=-1)
    lbar = jnp.broadcast_to(lbar[:, None, :], (N_SLABS, bsz, 2 * half))
    rows = S5_T * bsz
    full2 = lambda t: (0, 0)
    full3 = lambda t: (0, 0, 0)
    nbytes = (2 * rows * ch * 4 + 2 * rows * ch * 2 + rows * ch * 4 + N_SLABS * rows * 2 * half * 4
              + 2 * (wb.size + wcr.size + wci.size + w_glu.size) * 2 + 2 * lbar.size * 4 + 6 * rows * ch * 4)
    return pl.pallas_call(
        _s5_kernel,
        grid=(seq // S5_T,),
        in_specs=[
            pl.BlockSpec((bsz, S5_T, ch), lambda t: (0, t, cblk)),
            pl.BlockSpec(wb.shape, full3),
            pl.BlockSpec(lbar.shape, full3),
            pl.BlockSpec(wcr.shape, full3),
            pl.BlockSpec(wci.shape, full3),
            pl.BlockSpec((1, ch), full2),
            pl.BlockSpec(w_glu.shape, full2),
            pl.BlockSpec((1, ch), full2),
        ],
        out_specs=pl.BlockSpec((bsz, S5_T, ch), lambda t: (0, t, 0)),
        out_shape=jax.ShapeDtypeStruct((bsz, seq, ch), BF16),
        scratch_shapes=[pltpu.VMEM((ch // V7X_LANES, rows, V7X_LANES), F32),
                        pltpu.VMEM((N_SLABS, rows, 2 * half), F32),
                        pltpu.VMEM((N_SLABS, bsz, 2 * half), F32)],
        compiler_params=_params(("arbitrary",), nbytes),
        name="s5",
    )(u3, wb, lbar, wcr, wci, d_skip.reshape(1, ch), w_glu, b_glu.reshape(1, ch))


FFN_T = 64
FFN_TC = 512


def _ffn_kernel(x_ref, g_ref, wg_ref, wv_ref, cg_ref, cv_ref, wd_ref, *rest, taps, final):
    if final:
        fg_ref, o_ref, acc_ref, h_ref, carry_ref = rest
    else:
        o_ref, acc_ref, h_ref, carry_ref = rest
    bsz, t_len, d = x_ref.shape
    tm = bsz * t_len
    n_slab = d // V7X_LANES
    hist = (taps - 1) * bsz
    c = pl.program_id(1)

    def full_rows(ref):
        return jnp.concatenate([ref[s] for s in range(n_slab)], axis=-1)

    @pl.when(jnp.logical_and(pl.program_id(0) == 0, c == 0))
    def _():
        carry_ref[...] = jnp.zeros(carry_ref.shape, F32)

    @pl.when(c == 0)
    def _():
        _to_time_major(x_ref[...], acc_ref, 0)
        h_ref[...] = _rmsnorm(full_rows(acc_ref), g_ref[...]).astype(BF16)

    h = h_ref[...]

    def conv(u, cw, slot):
        ext = jnp.concatenate([carry_ref[c, slot], u], axis=0)
        carry_ref[c, slot] = u[tm - hist:tm]
        out = cw[0:1, :] * ext[0:tm]
        for k in range(1, taps):
            out = out + cw[k:k + 1, :] * ext[k * bsz:k * bsz + tm]
        return out

    gate = conv(_dot(h, wg_ref[...]), cg_ref[...], 0)
    val = conv(_dot(h, wv_ref[...]), cv_ref[...], 1)
    act = (gate * _sigmoid(gate) * val).astype(BF16)
    part = _dot(act, wd_ref[...])
    for s in range(n_slab):
        acc_ref[s] += part[:, s * V7X_LANES:(s + 1) * V7X_LANES]

    @pl.when(c == pl.num_programs(1) - 1)
    def _():
        if final:
            y = _rmsnorm(full_rows(acc_ref), fg_ref[...])
            for s in range(n_slab):
                acc_ref[s] = y[:, s * V7X_LANES:(s + 1) * V7X_LANES]
        _from_time_major(acc_ref, o_ref)


def _ffn(x3, gain, w_up, conv_w, w_down, final_gain=None):
    bsz, seq, d = x3.shape
    taps = conv_w.shape[0]
    d_ff = w_down.shape[0]
    assert (taps - 1 <= FFN_T and seq % FFN_T == 0 and d_ff % FFN_TC == 0
            and bsz % V7X_SUBLANES == 0 and d % V7X_LANES == 0)
    nc = d_ff // FFN_TC
    tm = FFN_T * bsz
    hist = (taps - 1) * bsz
    final = final_gain is not None
    in_specs = [
        pl.BlockSpec((bsz, FFN_T, d), lambda i, c: (0, i, 0)),
        pl.BlockSpec((1, d), lambda i, c: (0, 0)),
        pl.BlockSpec((d, FFN_TC), lambda i, c: (0, c)),
        pl.BlockSpec((d, FFN_TC), lambda i, c: (0, nc + c)),
        pl.BlockSpec((taps, FFN_TC), lambda i, c: (0, c)),
        pl.BlockSpec((taps, FFN_TC), lambda i, c: (0, nc + c)),
        pl.BlockSpec((FFN_TC, d), lambda i, c: (c, 0)),
    ]
    args = [x3, gain.reshape(1, d), w_up, w_up, conv_w, conv_w, w_down]
    if final:
        in_specs.append(pl.BlockSpec((1, d), lambda i, c: (0, 0)))
        args.append(final_gain.reshape(1, d))
    nbytes = (5 * tm * d * 4 + tm * d * 2 + 2 * 3 * d * FFN_TC * 2 + nc * 2 * hist * FFN_TC * 4
              + 6 * (tm + hist) * FFN_TC * 4)
    return pl.pallas_call(
        functools.partial(_ffn_kernel, taps=taps, final=final),
        grid=(seq // FFN_T, nc),
        in_specs=in_specs,
        out_specs=pl.BlockSpec((bsz, FFN_T, d), lambda i, c: (0, i, 0)),
        out_shape=jax.ShapeDtypeStruct((bsz, seq, d), F32),
        scratch_shapes=[pltpu.VMEM((d // V7X_LANES, tm, V7X_LANES), F32),
                        pltpu.VMEM((tm, d), BF16),
                        pltpu.VMEM((nc, 2, hist, FFN_TC), F32)],
        compiler_params=_params(("arbitrary", "arbitrary"), nbytes),
        name="ffn",
    )(*args)


ATT_T = 256
ATT_G = 4
LOG2_E = 1.4426950408889634
F32_EXP2_UNDERFLOW = -150.0


def _attn_kernel(q_ref, k_ref, v_ref, tri_ref, o_ref, acc_ref, carry_ref):
    blk = tri_ref.shape[0]
    n_g = acc_ref.shape[0]
    dh = acc_ref.shape[1]
    qi = pl.program_id(2)
    tri = tri_ref[...]

    acc_ref[...] = jnp.zeros(acc_ref.shape, F32)
    carry_ref[...] = jnp.zeros(carry_ref.shape, F32)

    def process(chains):
        def rows(kb):
            return pl.ds(pl.multiple_of(kb * blk, blk), blk)
        cols = lambda g: slice(g * dh, (g + 1) * dh)
        zs = [lax.dot_general(k_ref[rows(kb), cols(g)], q_ref[:, cols(g)], (((1,), (1,)), ((), ())),
                              preferred_element_type=F32) for g, kb, _ in chains]
        before = (lax.broadcasted_iota(jnp.int32, (blk, blk), 0) < lax.broadcasted_iota(jnp.int32, (blk, blk), 1))
        parts = []
        for (_, _, diagonal), z in zip(chains, zs):
            log_beta = jnp.minimum(z, 0.0) - jnp.log(1.0 + jnp.exp2(-jnp.abs(z))) * LOG2_E
            log_keep = log_beta - z
            if diagonal:
                log_keep = jnp.where(before, log_keep, 0.0)
            parts.append((log_beta, log_keep))
        sufs = [_dot(tri, log_keep.astype(BF16)) for _, log_keep in parts]
        pvs = []
        for (g, kb, diagonal), (log_beta, _), suffix in zip(chains, parts, sufs):
            w = jnp.exp2(log_beta + suffix)
            if diagonal:
                w = jnp.where(before, w, 0.0)
            pvs.append(lax.dot_general(v_ref[rows(kb), cols(g)], w.astype(BF16), (((0,), (0,)), ((), ())),
                                       preferred_element_type=F32))
        for (g, _, _), (_, log_keep), pv in zip(chains, parts, pvs):
            carry = carry_ref[g]
            acc_ref[g] += jnp.exp2(carry) * pv
            carry_ref[g] = carry + jnp.sum(log_keep, axis=0, keepdims=True)

    diag_chains = [(g, qi, True) for g in range(n_g)]

    @pl.when(qi == 0)
    def _():
        process(diag_chains)

    @pl.when(qi > 0)
    def _():
        process(diag_chains + [(g, qi - 1, False) for g in range(n_g)])

    def cond(state):
        kb, max_carry = state
        return jnp.logical_and(kb >= 0, max_carry > F32_EXP2_UNDERFLOW)

    def body(state):
        kb, _ = state
        process([(g, kb, False) for g in range(n_g)])
        return kb - 1, jnp.max(carry_ref[...])

    lax.while_loop(cond, body, (qi - 2, jnp.max(carry_ref[...])))
    for g in range(n_g):
        o_ref[:, g * dh:(g + 1) * dh] = acc_ref[g].T.astype(o_ref.dtype)


def _attention(qkv, bsz, seq):
    d = qkv.shape[1] // 3
    dh = d // N_HEADS
    assert seq % ATT_T == 0 and N_HEADS % ATT_G == 0
    nq = seq // ATT_T
    hg = N_HEADS // ATT_G
    gw = ATT_G * dh
    idx = jnp.arange(ATT_T)
    tri = (idx[None, :] > idx[:, None]).astype(BF16)
    nbytes = (2 * ATT_T * gw * 2 + 2 * 2 * seq * gw * 2 + 2 * ATT_T * ATT_T * 2 + 2 * ATT_T * gw * 2
              + ATT_G * (dh + V7X_SUBLANES) * ATT_T * 4 + 2 * ATT_G * 10 * ATT_T * ATT_T * 4)
    return pl.pallas_call(
        _attn_kernel,
        grid=(bsz, hg, nq),
        in_specs=[
            pl.BlockSpec((ATT_T, gw), lambda b, h, i: (b * nq + i, h)),
            pl.BlockSpec((seq, gw), lambda b, h, i: (b, hg + h)),
            pl.BlockSpec((seq, gw), lambda b, h, i: (b, 2 * hg + h)),
            pl.BlockSpec((ATT_T, ATT_T), lambda b, h, i: (0, 0)),
        ],
        out_specs=pl.BlockSpec((ATT_T, gw), lambda b, h, i: (b * nq + i, h)),
        out_shape=jax.ShapeDtypeStruct((bsz * seq, d), BF16),
        scratch_shapes=[pltpu.VMEM((ATT_G, dh, ATT_T), F32), pltpu.VMEM((ATT_G, 1, ATT_T), F32)],
        compiler_params=_params(("parallel", "parallel", "arbitrary"), nbytes),
        name="attention",
    )(qkv, qkv, qkv, tri)


PROJ_TM = 1024
PROJ_TN = 1024


def kernel(x, ln_mix_even, w_in, conv_w, conv_b, conv_ln_g, conv_ln_b, ssm_lam_re, ssm_lam_im, ssm_log_step, ssm_b_re, ssm_b_im, ssm_c_re, ssm_c_im, ssm_d, ssm_w_glu, ssm_b_glu, w_out_even, ln_mix_odd, w_qkv, w_o, ln_ffn, ffn_w_up, ffn_conv_w, ffn_w_down, ln_final):
    bsz, seq, d = x.shape
    depth = ln_ffn.shape[0]
    assert depth == 2 and ln_mix_even.shape[0] == 1 and ln_mix_odd.shape[0] == 1
    tm = min(PROJ_TM, seq)
    tn = min(PROJ_TN, d)
    d_conv = conv_w.shape[-1]
    d_ssm = ssm_w_glu.shape[-1]
    d_in = w_in.shape[-1]
    assert d_conv == d_ssm and d_in == 2 * d_conv + d_ssm
    x2d = x.reshape(bsz * seq, d)

    u = _norm_matmul(x2d, ln_mix_even[0], w_in[0].astype(BF16), F32, tm=tm, tn=tn, name="in_proj")
    u3 = u.reshape(bsz, seq, d_in)
    ya = _conformer(u3, conv_w[0], conv_b[0], conv_ln_g[0], conv_ln_b[0])
    lbar_re, lbar_im, bb_re, bb_im = _zoh(ssm_lam_re[0], ssm_lam_im[0], ssm_log_step[0], ssm_b_re[0], ssm_b_im[0])
    yb = _s5(u3, 2, lbar_re, lbar_im, bb_re, bb_im, ssm_c_re[0], ssm_c_im[0], ssm_d[0],
             ssm_w_glu[0].astype(BF16), ssm_b_glu[0])
    x2d = _res_matmul([ya.reshape(bsz * seq, d_conv), yb.reshape(bsz * seq, d_ssm)], w_out_even[0].astype(BF16),
                      x2d, tm=tm, tn=tn, name="out_proj")
    x2d = _ffn(x2d.reshape(bsz, seq, d), ln_ffn[0], ffn_w_up[0].astype(BF16), ffn_conv_w[0],
               ffn_w_down[0].astype(BF16)).reshape(bsz * seq, d)

    dh = d // N_HEADS
    col_scale = jnp.concatenate([jnp.full((d,), dh ** -0.5 * LOG2_E, F32), jnp.ones((2 * d,), F32)])
    qkv = _norm_matmul(x2d, ln_mix_odd[0], w_qkv[0].astype(BF16), BF16, tm=tm, tn=tn, col_scale=col_scale,
                       name="qkv_proj")
    o = _attention(qkv, bsz, seq)
    x2d = _res_matmul([o], w_o[0].astype(BF16), x2d, tm=tm, tn=tn, name="o_proj")
    return _ffn(x2d.reshape(bsz, seq, d), ln_ffn[1], ffn_w_up[1].astype(BF16), ffn_conv_w[1],
                ffn_w_down[1].astype(BF16), final_gain=ln_final)
```

```python
import functools

import jax
import jax.numpy as jnp
from jax import lax
from jax.experimental import pallas as pl
from jax.experimental.pallas import tpu as pltpu

F32 = jnp.float32
BF16 = jnp.bfloat16

EPS = 1e-6

V7X_SUBLANES = 8
V7X_LANES = 128
V7X_VMEM_BYTES = 64 * 1024 * 1024

N_HEADS = 16
N_SLABS = 8


def _vmem_limit(nbytes):
    return int(min(max(nbytes * 3 // 2, 16 * 1024 * 1024), V7X_VMEM_BYTES * 7 // 8))


def _params(semantics, nbytes):
    return pltpu.CompilerParams(dimension_semantics=semantics, vmem_limit_bytes=_vmem_limit(nbytes))


def _dot(a, b):
    return jnp.dot(a, b, preferred_element_type=F32)


def _rmsnorm(x, g):
    ms = jnp.mean(x * x, axis=-1, keepdims=True)
    return x * lax.rsqrt(ms + EPS) * g


def _sigmoid(x):
    return 1.0 / (1.0 + jnp.exp(-x))


def _to_time_major(src, dst_ref, row0):
    bsz, t_len, ch = src.shape
    for b in range(bsz):
        for s in range(ch // V7X_LANES):
            dst_ref[s, pl.ds(row0 + b, t_len, stride=bsz), :] = src[b, :, s * V7X_LANES:(s + 1) * V7X_LANES]


def _from_time_major(src_ref, o_ref):
    bsz, t_len, ch = o_ref.shape
    for b in range(bsz):
        for s in range(ch // V7X_LANES):
            o_ref[b, :, s * V7X_LANES:(s + 1) * V7X_LANES] = (
                src_ref[s, pl.ds(b, t_len, stride=bsz), :].astype(o_ref.dtype))


CAST_TR = 512
CAST_TC_MAX = 4096


def _cast_kernel(x_ref, o_ref):
    o_ref[...] = x_ref[...].astype(o_ref.dtype)


def _to_bf16(w3):
    n, r, c = w3.shape
    tr = min(r, CAST_TR)
    tc = max(t for t in range(V7X_LANES, min(c, CAST_TC_MAX) + 1, V7X_LANES) if c % t == 0)
    assert r % tr == 0
    spec = pl.BlockSpec((None, tr, tc), lambda l, i, j: (l, i, j))
    return pl.pallas_call(
        _cast_kernel,
        grid=(n, r // tr, c // tc),
        in_specs=[spec],
        out_specs=spec,
        out_shape=jax.ShapeDtypeStruct(w3.shape, BF16),
        compiler_params=_params(("parallel", "parallel", "parallel"), 2 * tr * tc * 6),
        name="cast",
    )(w3)


def _zoh_kernel(lr_ref, li_ref, ls_ref, br_ref, bi_ref, are_ref, aim_ref, bbr_ref, bbi_ref):
    lr = lr_ref[...]
    li = li_ref[...]
    step = jnp.exp(ls_ref[...])
    mag = jnp.exp(lr * step)
    lbar_re = mag * jnp.cos(li * step)
    lbar_im = mag * jnp.sin(li * step)
    num_re = lbar_re - 1.0
    num_im = lbar_im
    den = lr * lr + li * li
    f_re = (num_re * lr + num_im * li) / den
    f_im = (num_im * lr - num_re * li) / den
    br = br_ref[...]
    bi = bi_ref[...]
    are_ref[...] = lbar_re
    aim_ref[...] = lbar_im
    bbr_ref[...] = f_re * br - f_im * bi
    bbi_ref[...] = f_re * bi + f_im * br


def _zoh(lam_re, lam_im, log_step, b_re, b_im):
    g, p = lam_re.shape
    h = b_re.shape[-1]
    rep = lambda a: jnp.broadcast_to(a[:, None, :], (g, h, p)).reshape(g * h, p)
    args = (rep(lam_re), rep(lam_im), rep(jnp.broadcast_to(log_step[:, None], (g, p))),
            jnp.transpose(b_re, (0, 2, 1)).reshape(g * h, p),
            jnp.transpose(b_im, (0, 2, 1)).reshape(g * h, p))
    out = pl.pallas_call(
        _zoh_kernel,
        out_shape=[jax.ShapeDtypeStruct((g * h, p), F32)] * 4,
        name="zoh",
    )(*args)
    are, aim, bbr, bbi = [o.reshape(g, h, p) for o in out]
    return are[:, 0], aim[:, 0], bbr, bbi


def _norm_matmul_kernel(x_ref, g_ref, w_ref, *rest, has_scale):
    if has_scale:
        s_ref, o_ref, h_ref = rest
    else:
        o_ref, h_ref = rest

    @pl.when(pl.program_id(1) == 0)
    def _():
        h_ref[...] = _rmsnorm(x_ref[...], g_ref[...]).astype(BF16)

    acc = _dot(h_ref[...], w_ref[...])
    if has_scale:
        acc = acc * s_ref[...]
    o_ref[...] = acc.astype(o_ref.dtype)


def _norm_matmul(x2d, gain, w, out_dtype, *, tm, tn, col_scale=None, name):
    n_rows = x2d.shape[0]
    d, n = w.shape
    in_specs = [
        pl.BlockSpec((tm, d), lambda i, j: (i, 0)),
        pl.BlockSpec((1, d), lambda i, j: (0, 0)),
        pl.BlockSpec((d, tn), lambda i, j: (0, j)),
    ]
    args = [x2d, gain.reshape(1, d), w]
    if col_scale is not None:
        in_specs.append(pl.BlockSpec((1, tn), lambda i, j: (0, j)))
        args.append(col_scale.reshape(1, n))
    out_bytes = jnp.dtype(out_dtype).itemsize
    nbytes = 2 * tm * d * 4 + tm * d * 2 + 2 * d * tn * 2 + 2 * tm * tn * out_bytes + tm * tn * 4
    return pl.pallas_call(
        functools.partial(_norm_matmul_kernel, has_scale=col_scale is not None),
        grid=(n_rows // tm, n // tn),
        in_specs=in_specs,
        out_specs=pl.BlockSpec((tm, tn), lambda i, j: (i, j)),
        out_shape=jax.ShapeDtypeStruct((n_rows, n), out_dtype),
        scratch_shapes=[pltpu.VMEM((tm, d), BF16)],
        compiler_params=_params(("parallel", "arbitrary"), nbytes),
        name=name,
    )(*args)


def _res_matmul_kernel(*refs, n_a):
    a_refs = refs[:n_a]
    w_refs = refs[n_a:2 * n_a]
    res_ref, o_ref = refs[2 * n_a:]
    acc = res_ref[...]
    for a_ref, w_ref in zip(a_refs, w_refs):
        acc = acc + _dot(a_ref[...], w_ref[...])
    o_ref[...] = acc


def _res_matmul(a_list, w, res2d, *, tm, tn, name):
    n_a = len(a_list)
    n_rows, ka = a_list[0].shape
    assert n_a * ka == w.shape[0]
    n = w.shape[1]
    in_specs = [pl.BlockSpec((tm, ka), lambda i, j: (i, 0)) for _ in a_list]
    in_specs += [pl.BlockSpec((ka, tn), lambda i, j, r=r: (r, j)) for r in range(n_a)]
    in_specs += [pl.BlockSpec((tm, tn), lambda i, j: (i, j))]
    nbytes = 2 * tm * n_a * ka * 2 + 2 * n_a * ka * tn * 2 + 4 * tm * tn * 4 + tm * tn * 4
    return pl.pallas_call(
        functools.partial(_res_matmul_kernel, n_a=n_a),
        grid=(n_rows // tm, n // tn),
        in_specs=in_specs,
        out_specs=pl.BlockSpec((tm, tn), lambda i, j: (i, j)),
        out_shape=jax.ShapeDtypeStruct((n_rows, n), F32),
        compiler_params=_params(("parallel", "arbitrary"), nbytes),
        name=name,
    )(*a_list, *([w] * n_a), res2d)


CONV_T = 64
CONV_HALO = 32
CONV_TB = 16


def _conformer_kernel(val_ref, gate_ref, hval_ref, hgate_ref, wb_ref, cb_ref, lng_ref, lnb_ref, o_ref,
                      hx_ref, c_ref, *, taps):
    bsz, t_len, ch = val_ref.shape
    n_slab = ch // V7X_LANES
    keep = (pl.program_id(0) > 0).astype(F32)
    _to_time_major(hval_ref[...] * _sigmoid(hgate_ref[...]) * keep, hx_ref, 0)
    _to_time_major(val_ref[...] * _sigmoid(gate_ref[...]), hx_ref, CONV_HALO * bsz)

    lead = CONV_HALO - (taps - 1)
    for s in range(n_slab):
        def body(tb, carry, s=s):
            t0 = tb * CONV_TB
            acc = jnp.zeros((CONV_TB, bsz, V7X_LANES), F32)
            for k in range(taps):
                r0 = pl.multiple_of((t0 + lead + k) * bsz, bsz)
                rows = hx_ref[s, pl.ds(r0, CONV_TB * bsz), :].reshape(CONV_TB, bsz, V7X_LANES)
                acc = acc + wb_ref[k, :, s * V7X_LANES:(s + 1) * V7X_LANES][None] * rows
            c_ref[s, pl.ds(pl.multiple_of(t0 * bsz, bsz), CONV_TB * bsz), :] = acc.reshape(CONV_TB * bsz, V7X_LANES)
            return carry
        lax.fori_loop(0, t_len // CONV_TB, body, 0)

    h = jnp.concatenate([c_ref[s] for s in range(n_slab)], axis=-1) + cb_ref[...]
    mu = jnp.mean(h, axis=-1, keepdims=True)
    dlt = h - mu
    var = jnp.mean(dlt * dlt, axis=-1, keepdims=True)
    y = dlt * lax.rsqrt(var + EPS) * lng_ref[...] + lnb_ref[...]
    y = y * _sigmoid(y)
    for s in range(n_slab):
        c_ref[s] = y[:, s * V7X_LANES:(s + 1) * V7X_LANES]
    _from_time_major(c_ref, o_ref)


def _conformer(u3, conv_w, conv_b, ln_g, ln_b):
    bsz, seq, _ = u3.shape
    taps, ch = conv_w.shape
    assert taps - 1 <= CONV_HALO and CONV_T % CONV_HALO == 0 and seq % CONV_T == 0 and ch % V7X_LANES == 0
    wb = jnp.broadcast_to(conv_w[:, None, :], (taps, bsz, ch))
    ratio = CONV_T // CONV_HALO
    halo_idx = lambda c: (lambda t: (0, jnp.maximum(t * ratio - 1, 0), c))
    row = lambda a: a.reshape(1, ch)
    full2 = lambda t: (0, 0)
    nbytes = (2 * 2 * (CONV_T + CONV_HALO) * bsz * ch * 4 + 2 * taps * bsz * ch * 4
              + (2 * CONV_T + CONV_HALO) * bsz * ch * 4 + 2 * CONV_T * bsz * ch * 2 + 6 * CONV_T * bsz * ch * 4)
    return pl.pallas_call(
        functools.partial(_conformer_kernel, taps=taps),
        grid=(seq // CONV_T,),
        in_specs=[
            pl.BlockSpec((bsz, CONV_T, ch), lambda t: (0, t, 0)),
            pl.BlockSpec((bsz, CONV_T, ch), lambda t: (0, t, 1)),
            pl.BlockSpec((bsz, CONV_HALO, ch), halo_idx(0)),
            pl.BlockSpec((bsz, CONV_HALO, ch), halo_idx(1)),
            pl.BlockSpec((taps, bsz, ch), lambda t: (0, 0, 0)),
            pl.BlockSpec((1, ch), full2),
            pl.BlockSpec((1, ch), full2),
            pl.BlockSpec((1, ch), full2),
        ],
        out_specs=pl.BlockSpec((bsz, CONV_T, ch), lambda t: (0, t, 0)),
        out_shape=jax.ShapeDtypeStruct((bsz, seq, ch), BF16),
        scratch_shapes=[pltpu.VMEM((ch // V7X_LANES, (CONV_HALO + CONV_T) * bsz, V7X_LANES), F32),
                        pltpu.VMEM((ch // V7X_LANES, CONV_T * bsz, V7X_LANES), F32)],
        compiler_params=_params(("parallel",), nbytes),
        name="conformer",
    )(u3, u3, u3, u3, wb, row(conv_b), row(ln_g), row(ln_b))


S5_T = 32


def _s5_kernel(u_ref, wb_ref, lbar_ref, wcr_ref, wci_ref, d_ref, wg_ref, bg_ref, o_ref, ut_ref, bu_ref, st_ref):
    bsz, t_len, ch = u_ref.shape
    rows = t_len * bsz
    n_slabs, _, width = lbar_ref.shape
    half = width // 2
    cin = wb_ref.shape[1]

    @pl.when(pl.program_id(0) == 0)
    def _():
        st_ref[...] = jnp.zeros(st_ref.shape, F32)

    _to_time_major(u_ref[...], ut_ref, 0)
    u = jnp.concatenate([ut_ref[s] for s in range(ch // V7X_LANES)], axis=-1)
    ub = u.astype(BF16)
    for j in range(n_slabs):
        bu_ref[j] = _dot(ub[:, j * cin:(j + 1) * cin], wb_ref[j])

    ys = []
    for j in range(n_slabs):
        a = lbar_ref[j]
        ar, ai = a[:, :half], a[:, half:]
        s = st_ref[j]
        xr, xi = s[:, :half], s[:, half:]
        for t in range(t_len):
            b = bu_ref[j, t * bsz:(t + 1) * bsz, :]
            xr, xi = ar * xr - ai * xi + b[:, :half], ar * xi + ai * xr + b[:, half:]
            bu_ref[j, t * bsz:(t + 1) * bsz, :] = jnp.concatenate([xr, xi], axis=-1)
        st_ref[j] = jnp.concatenate([xr, xi], axis=-1)
        s = bu_ref[j]
        ys.append(_dot(s[:, :half].astype(BF16), wcr_ref[j]) - _dot(s[:, half:].astype(BF16), wci_ref[j]))
    y = jnp.concatenate(ys, axis=-1) + d_ref[...] * u
    y = jax.nn.gelu(y)
    gate = _sigmoid(_dot(y.astype(BF16), wg_ref[...]) + bg_ref[...])
    y = y * gate
    for s in range(ch // V7X_LANES):
        ut_ref[s] = y[:, s * V7X_LANES:(s + 1) * V7X_LANES]
    _from_time_major(ut_ref, o_ref)


def _s5(u3, cblk, lbar_re, lbar_im, bb_re, bb_im, c_re, c_im, d_skip, w_glu, b_glu):
    bsz, seq, _ = u3.shape
    g, h, p = bb_re.shape
    ch = g * h
    gs = g // N_SLABS
    half = gs * p
    eye = jnp.eye(gs, dtype=F32)

    def place_b(bb):
        return jnp.einsum("jghp,gk->jghkp", bb.reshape(N_SLABS, gs, h, p), eye).reshape(N_SLABS, gs * h, half)

    def place_c(c):
        return jnp.einsum("jghp,gk->jgpkh", c.reshape(N_SLABS, gs, h, p), eye).reshape(N_SLABS, half, gs * h)

    wb = jnp.concatenate([place_b(bb_re), place_b(bb_im)], axis=-1).astype(BF16)
    wcr = place_c(c_re).astype(BF16)
    wci = place_c(c_im).astype(BF16)
    lbar = jnp.concatenate([lbar_re.reshape(N_SLABS, half), lbar_im.reshape(N_SLABS, half)], axis=-1)
    lbar = jnp.broadcast_to(lbar[:, None, :], (N_SLABS, bsz, 2 * half))
    rows = S5_T * bsz
    full2 = lambda t: (0, 0)
    full3 = lambda t: (0, 0, 0)
    nbytes = (2 * rows * ch * 4 + 2 * rows * ch * 2 + rows * ch * 4 + N_SLABS * rows * 2 * half * 4
              + 2 * (wb.size + wcr.size + wci.size + w_glu.size) * 2 + 2 * lbar.size * 4 + 6 * rows * ch * 4)
    return pl.pallas_call(
        _s5_kernel,
        grid=(seq // S5_T,),
        in_specs=[
            pl.BlockSpec((bsz, S5_T, ch), lambda t: (0, t, cblk)),
            pl.BlockSpec(wb.shape, full3),
            pl.BlockSpec(lbar.shape, full3),
            pl.BlockSpec(wcr.shape, full3),
            pl.BlockSpec(wci.shape, full3),
            pl.BlockSpec((1, ch), full2),
            pl.BlockSpec(w_glu.shape, full2),
            pl.BlockSpec((1, ch), full2),
        ],
        out_specs=pl.BlockSpec((bsz, S5_T, ch), lambda t: (0, t, 0)),
        out_shape=jax.ShapeDtypeStruct((bsz, seq, ch), BF16),
        scratch_shapes=[pltpu.VMEM((ch // V7X_LANES, rows, V7X_LANES), F32),
                        pltpu.VMEM((N_SLABS, rows, 2 * half), F32),
                        pltpu.VMEM((N_SLABS, bsz, 2 * half), F32)],
        compiler_params=_params(("arbitrary",), nbytes),
        name="s5",
    )(u3, wb, lbar, wcr, wci, d_skip.reshape(1, ch), w_glu, b_glu.reshape(1, ch))


FFN_T = 64
FFN_TC = 512


def _ffn_kernel(x_ref, g_ref, wg_ref, wv_ref, cg_ref, cv_ref, wd_ref, *rest, taps, final):
    if final:
        fg_ref, o_ref, acc_ref, h_ref, carry_ref = rest
    else:
        o_ref, acc_ref, h_ref, carry_ref = rest
    bsz, t_len, d = x_ref.shape
    tm = bsz * t_len
    n_slab = d // V7X_LANES
    hist = (taps - 1) * bsz
    c = pl.program_id(1)

    def full_rows(ref):
        return jnp.concatenate([ref[s] for s in range(n_slab)], axis=-1)

    @pl.when(jnp.logical_and(pl.program_id(0) == 0, c == 0))
    def _():
        carry_ref[...] = jnp.zeros(carry_ref.shape, F32)

    @pl.when(c == 0)
    def _():
        _to_time_major(x_ref[...], acc_ref, 0)
        h_ref[...] = _rmsnorm(full_rows(acc_ref), g_ref[...]).astype(BF16)

    h = h_ref[...]

    def conv(u, cw, slot):
        ext = jnp.concatenate([carry_ref[c, slot], u], axis=0)
        carry_ref[c, slot] = u[tm - hist:tm]
        out = cw[0:1, :] * ext[0:tm]
        for k in range(1, taps):
            out = out + cw[k:k + 1, :] * ext[k * bsz:k * bsz + tm]
        return out

    gate = conv(_dot(h, wg_ref[...]), cg_ref[...], 0)
    val = conv(_dot(h, wv_ref[...]), cv_ref[...], 1)
    act = (gate * _sigmoid(gate) * val).astype(BF16)
    part = _dot(act, wd_ref[...])
    for s in range(n_slab):
        acc_ref[s] += part[:, s * V7X_LANES:(s + 1) * V7X_LANES]

    @pl.when(c == pl.num_programs(1) - 1)
    def _():
        if final:
            y = _rmsnorm(full_rows(acc_ref), fg_ref[...])
            for s in range(n_slab):
                acc_ref[s] = y[:, s * V7X_LANES:(s + 1) * V7X_LANES]
        _from_time_major(acc_ref, o_ref)


def _ffn(x3, gain, w_up, conv_w, w_down, layer, final_gain=None):
    bsz, seq, d = x3.shape
    taps = conv_w.shape[0]
    d_ff = w_down.shape[1]
    assert (taps - 1 <= FFN_T and seq % FFN_T == 0 and d_ff % FFN_TC == 0
            and bsz % V7X_SUBLANES == 0 and d % V7X_LANES == 0)
    nc = d_ff // FFN_TC
    tm = FFN_T * bsz
    hist = (taps - 1) * bsz
    final = final_gain is not None
    in_specs = [
        pl.BlockSpec((bsz, FFN_T, d), lambda i, c: (0, i, 0)),
        pl.BlockSpec((1, d), lambda i, c: (0, 0)),
        pl.BlockSpec((None, d, FFN_TC), lambda i, c: (layer, 0, c)),
        pl.BlockSpec((None, d, FFN_TC), lambda i, c: (layer, 0, nc + c)),
        pl.BlockSpec((taps, FFN_TC), lambda i, c: (0, c)),
        pl.BlockSpec((taps, FFN_TC), lambda i, c: (0, nc + c)),
        pl.BlockSpec((None, FFN_TC, d), lambda i, c: (layer, c, 0)),
    ]
    args = [x3, gain.reshape(1, d), w_up, w_up, conv_w, conv_w, w_down]
    if final:
        in_specs.append(pl.BlockSpec((1, d), lambda i, c: (0, 0)))
        args.append(final_gain.reshape(1, d))
    nbytes = (5 * tm * d * 4 + tm * d * 2 + 2 * 3 * d * FFN_TC * 2 + nc * 2 * hist * FFN_TC * 4
              + 6 * (tm + hist) * FFN_TC * 4)
    return pl.pallas_call(
        functools.partial(_ffn_kernel, taps=taps, final=final),
        grid=(seq // FFN_T, nc),
        in_specs=in_specs,
        out_specs=pl.BlockSpec((bsz, FFN_T, d), lambda i, c: (0, i, 0)),
        out_shape=jax.ShapeDtypeStruct((bsz, seq, d), F32),
        scratch_shapes=[pltpu.VMEM((d // V7X_LANES, tm, V7X_LANES), F32),
                        pltpu.VMEM((tm, d), BF16),
                        pltpu.VMEM((nc, 2, hist, FFN_TC), F32)],
        compiler_params=_params(("arbitrary", "arbitrary"), nbytes),
        name="ffn",
    )(*args)


ATT_T = 256
ATT_G = 4
LOG2_E = 1.4426950408889634
F32_EXP2_UNDERFLOW = -150.0


def _attn_kernel(q_ref, k_ref, v_ref, tri_ref, o_ref, acc_ref, carry_ref):
    blk = tri_ref.shape[0]
    n_g = acc_ref.shape[0]
    dh = acc_ref.shape[1]
    n_q = q_ref.shape[0] // blk
    tri = tri_ref[...]

    def rows(kb):
        return pl.ds(pl.multiple_of(kb * blk, blk), blk)

    def process(qi, chains):
        cols = lambda g: slice(g * dh, (g + 1) * dh)
        zs = [lax.dot_general(k_ref[rows(kb), cols(g)], q_ref[rows(qi), cols(g)], (((1,), (1,)), ((), ())),
                              preferred_element_type=F32) for g, kb, _ in chains]
        before = (lax.broadcasted_iota(jnp.int32, (blk, blk), 0) < lax.broadcasted_iota(jnp.int32, (blk, blk), 1))
        parts = []
        for (_, _, diagonal), z in zip(chains, zs):
            log_beta = jnp.minimum(z, 0.0) - jnp.log(1.0 + jnp.exp2(-jnp.abs(z))) * LOG2_E
            log_keep = log_beta - z
            if diagonal:
                log_keep = jnp.where(before, log_keep, 0.0)
            parts.append((log_beta, log_keep))
        sufs = [_dot(tri, log_keep.astype(BF16)) for _, log_keep in parts]
        pvs = []
        for (g, kb, diagonal), (log_beta, _), suffix in zip(chains, parts, sufs):
            w = jnp.exp2(log_beta + suffix)
            if diagonal:
                w = jnp.where(before, w, 0.0)
            pvs.append(lax.dot_general(v_ref[rows(kb), cols(g)], w.astype(BF16), (((0,), (0,)), ((), ())),
                                       preferred_element_type=F32))
        for (g, _, _), (_, log_keep), pv in zip(chains, parts, pvs):
            carry = carry_ref[g]
            acc_ref[g] += jnp.exp2(carry) * pv
            carry_ref[g] = carry + jnp.sum(log_keep, axis=0, keepdims=True)

    def reset():
        acc_ref[...] = jnp.zeros(acc_ref.shape, F32)
        carry_ref[...] = jnp.zeros(carry_ref.shape, F32)

    def emit(qi):
        for g in range(n_g):
            o_ref[rows(qi), g * dh:(g + 1) * dh] = acc_ref[g].T.astype(o_ref.dtype)

    reset()
    process(0, [(g, 0, True) for g in range(n_g)])
    emit(0)

    def query_block(qi, state):
        reset()
        process(qi, [(g, qi, True) for g in range(n_g)] + [(g, qi - 1, False) for g in range(n_g)])

        def cond(inner):
            kb, max_carry = inner
            return jnp.logical_and(kb >= 0, max_carry > F32_EXP2_UNDERFLOW)

        def body(inner):
            kb, _ = inner
            process(qi, [(g, kb, False) for g in range(n_g)])
            return kb - 1, jnp.max(carry_ref[...])

        lax.while_loop(cond, body, (qi - 2, jnp.max(carry_ref[...])))
        emit(qi)
        return state
    lax.fori_loop(1, n_q, query_block, 0)


def _attention(qkv, bsz, seq):
    d = qkv.shape[1] // 3
    dh = d // N_HEADS
    assert seq % ATT_T == 0 and N_HEADS % ATT_G == 0
    hg = N_HEADS // ATT_G
    gw = ATT_G * dh
    idx = jnp.arange(ATT_T)
    tri = (idx[None, :] > idx[:, None]).astype(BF16)
    nbytes = (2 * 4 * seq * gw * 2 + 2 * ATT_T * ATT_T * 2
              + ATT_G * (dh + V7X_SUBLANES) * ATT_T * 4 + 2 * ATT_G * 10 * ATT_T * ATT_T * 4)
    return pl.pallas_call(
        _attn_kernel,
        grid=(bsz, hg),
        in_specs=[
            pl.BlockSpec((seq, gw), lambda b, h: (b, h)),
            pl.BlockSpec((seq, gw), lambda b, h: (b, hg + h)),
            pl.BlockSpec((seq, gw), lambda b, h: (b, 2 * hg + h)),
            pl.BlockSpec((ATT_T, ATT_T), lambda b, h: (0, 0)),
        ],
        out_specs=pl.BlockSpec((seq, gw), lambda b, h: (b, h)),
        out_shape=jax.ShapeDtypeStruct((bsz * seq, d), BF16),
        scratch_shapes=[pltpu.VMEM((ATT_G, dh, ATT_T), F32), pltpu.VMEM((ATT_G, 1, ATT_T), F32)],
        compiler_params=_params(("parallel", "parallel"), nbytes),
        name="attention",
    )(qkv, qkv, qkv, tri)


PROJ_TM = 1024
PROJ_TN = 1024


def kernel(x, ln_mix_even, w_in, conv_w, conv_b, conv_ln_g, conv_ln_b, ssm_lam_re, ssm_lam_im, ssm_log_step, ssm_b_re, ssm_b_im, ssm_c_re, ssm_c_im, ssm_d, ssm_w_glu, ssm_b_glu, w_out_even, ln_mix_odd, w_qkv, w_o, ln_ffn, ffn_w_up, ffn_conv_w, ffn_w_down, ln_final):
    bsz, seq, d = x.shape
    depth = ln_ffn.shape[0]
    assert depth == 2 and ln_mix_even.shape[0] == 1 and ln_mix_odd.shape[0] == 1
    tm = min(PROJ_TM, seq)
    tn = min(PROJ_TN, d)
    d_conv = conv_w.shape[-1]
    d_ssm = ssm_w_glu.shape[-1]
    d_in = w_in.shape[-1]
    assert d_conv == d_ssm and d_in == 2 * d_conv + d_ssm
    x2d = x.reshape(bsz * seq, d)
    w_up_bf = _to_bf16(ffn_w_up)
    w_down_bf = _to_bf16(ffn_w_down)

    u = _norm_matmul(x2d, ln_mix_even[0], _to_bf16(w_in)[0], F32, tm=tm, tn=tn, name="in_proj")
    u3 = u.reshape(bsz, seq, d_in)
    ya = _conformer(u3, conv_w[0], conv_b[0], conv_ln_g[0], conv_ln_b[0])
    lbar_re, lbar_im, bb_re, bb_im = _zoh(ssm_lam_re[0], ssm_lam_im[0], ssm_log_step[0], ssm_b_re[0], ssm_b_im[0])
    yb = _s5(u3, 2, lbar_re, lbar_im, bb_re, bb_im, ssm_c_re[0], ssm_c_im[0], ssm_d[0],
             _to_bf16(ssm_w_glu)[0], ssm_b_glu[0])
    x2d = _res_matmul([ya.reshape(bsz * seq, d_conv), yb.reshape(bsz * seq, d_ssm)], _to_bf16(w_out_even)[0],
                      x2d, tm=tm, tn=tn, name="out_proj")
    x2d = _ffn(x2d.reshape(bsz, seq, d), ln_ffn[0], w_up_bf, ffn_conv_w[0], w_down_bf, 0).reshape(bsz * seq, d)

    dh = d // N_HEADS
    col_scale = jnp.concatenate([jnp.full((d,), dh ** -0.5 * LOG2_E, F32), jnp.ones((2 * d,), F32)])
    qkv = _norm_matmul(x2d, ln_mix_odd[0], _to_bf16(w_qkv)[0], BF16, tm=tm, tn=tn, col_scale=col_scale,
                       name="qkv_proj")
    o = _attention(qkv, bsz, seq)
    x2d = _res_matmul([o], _to_bf16(w_o)[0], x2d, tm=tm, tn=tn, name="o_proj")
    return _ffn(x2d.reshape(bsz, seq, d), ln_ffn[1], w_up_bf, ffn_conv_w[1], w_down_bf, 1, final_gain=ln_final)
```

```python
import functools

import jax
import jax.numpy as jnp
from jax import lax
from jax.experimental import pallas as pl
from jax.experimental.pallas import tpu as pltpu

F32 = jnp.float32
BF16 = jnp.bfloat16

EPS = 1e-6

V7X_SUBLANES = 8
V7X_LANES = 128
V7X_VMEM_BYTES = 64 * 1024 * 1024

N_HEADS = 16
N_SLABS = 8


def _vmem_limit(nbytes):
    return int(min(max(nbytes * 3 // 2, 16 * 1024 * 1024), V7X_VMEM_BYTES * 7 // 8))


def _params(semantics, nbytes):
    return pltpu.CompilerParams(dimension_semantics=semantics, vmem_limit_bytes=_vmem_limit(nbytes))


def _dot(a, b):
    return jnp.dot(a, b, preferred_element_type=F32)


def _rmsnorm(x, g):
    ms = jnp.mean(x * x, axis=-1, keepdims=True)
    return x * lax.rsqrt(ms + EPS) * g


def _sigmoid(x):
    return 1.0 / (1.0 + jnp.exp(-x))


def _to_time_major(src, dst_ref, row0):
    bsz, t_len, ch = src.shape
    for b in range(bsz):
        for s in range(ch // V7X_LANES):
            dst_ref[s, pl.ds(row0 + b, t_len, stride=bsz), :] = src[b, :, s * V7X_LANES:(s + 1) * V7X_LANES]


def _from_time_major(src_ref, o_ref):
    bsz, t_len, ch = o_ref.shape
    for b in range(bsz):
        for s in range(ch // V7X_LANES):
            o_ref[b, :, s * V7X_LANES:(s + 1) * V7X_LANES] = (
                src_ref[s, pl.ds(b, t_len, stride=bsz), :].astype(o_ref.dtype))


CAST_TR = 512
CAST_TC_MAX = 4096


def _cast_kernel(x_ref, o_ref):
    o_ref[...] = x_ref[...].astype(o_ref.dtype)


def _to_bf16(w3):
    n, r, c = w3.shape
    tr = min(r, CAST_TR)
    tc = max(t for t in range(V7X_LANES, min(c, CAST_TC_MAX) + 1, V7X_LANES) if c % t == 0)
    assert r % tr == 0
    spec = pl.BlockSpec((None, tr, tc), lambda l, i, j: (l, i, j))
    return pl.pallas_call(
        _cast_kernel,
        grid=(n, r // tr, c // tc),
        in_specs=[spec],
        out_specs=spec,
        out_shape=jax.ShapeDtypeStruct(w3.shape, BF16),
        compiler_params=_params(("parallel", "parallel", "parallel"), 2 * tr * tc * 6),
        name="cast",
    )(w3)


def _zoh_kernel(lr_ref, li_ref, ls_ref, br_ref, bi_ref, are_ref, aim_ref, bbr_ref, bbi_ref):
    lr = lr_ref[...]
    li = li_ref[...]
    step = jnp.exp(ls_ref[...])
    mag = jnp.exp(lr * step)
    lbar_re = mag * jnp.cos(li * step)
    lbar_im = mag * jnp.sin(li * step)
    num_re = lbar_re - 1.0
    num_im = lbar_im
    den = lr * lr + li * li
    f_re = (num_re * lr + num_im * li) / den
    f_im = (num_im * lr - num_re * li) / den
    br = br_ref[...]
    bi = bi_ref[...]
    are_ref[...] = lbar_re
    aim_ref[...] = lbar_im
    bbr_ref[...] = f_re * br - f_im * bi
    bbi_ref[...] = f_re * bi + f_im * br


def _zoh(lam_re, lam_im, log_step, b_re, b_im):
    g, p = lam_re.shape
    h = b_re.shape[-1]
    rep = lambda a: jnp.broadcast_to(a[:, None, :], (g, h, p)).reshape(g * h, p)
    args = (rep(lam_re), rep(lam_im), rep(jnp.broadcast_to(log_step[:, None], (g, p))),
            jnp.transpose(b_re, (0, 2, 1)).reshape(g * h, p),
            jnp.transpose(b_im, (0, 2, 1)).reshape(g * h, p))
    out = pl.pallas_call(
        _zoh_kernel,
        out_shape=[jax.ShapeDtypeStruct((g * h, p), F32)] * 4,
        name="zoh",
    )(*args)
    are, aim, bbr, bbi = [o.reshape(g, h, p) for o in out]
    return are[:, 0], aim[:, 0], bbr, bbi


def _norm_matmul_kernel(x_ref, g_ref, w_ref, *rest, has_scale):
    if has_scale:
        s_ref, o_ref, h_ref = rest
    else:
        o_ref, h_ref = rest

    @pl.when(pl.program_id(1) == 0)
    def _():
        h_ref[...] = _rmsnorm(x_ref[...], g_ref[...]).astype(BF16)

    acc = _dot(h_ref[...], w_ref[...])
    if has_scale:
        acc = acc * s_ref[...]
    o_ref[...] = acc.astype(o_ref.dtype)


def _norm_matmul(x2d, gain, w, out_dtype, *, tm, tn, col_scale=None, name):
    n_rows = x2d.shape[0]
    d, n = w.shape
    in_specs = [
        pl.BlockSpec((tm, d), lambda i, j: (i, 0)),
        pl.BlockSpec((1, d), lambda i, j: (0, 0)),
        pl.BlockSpec((d, tn), lambda i, j: (0, j)),
    ]
    args = [x2d, gain.reshape(1, d), w]
    if col_scale is not None:
        in_specs.append(pl.BlockSpec((1, tn), lambda i, j: (0, j)))
        args.append(col_scale.reshape(1, n))
    out_bytes = jnp.dtype(out_dtype).itemsize
    nbytes = 2 * tm * d * 4 + tm * d * 2 + 2 * d * tn * 2 + 2 * tm * tn * out_bytes + tm * tn * 4
    return pl.pallas_call(
        functools.partial(_norm_matmul_kernel, has_scale=col_scale is not None),
        grid=(n_rows // tm, n // tn),
        in_specs=in_specs,
        out_specs=pl.BlockSpec((tm, tn), lambda i, j: (i, j)),
        out_shape=jax.ShapeDtypeStruct((n_rows, n), out_dtype),
        scratch_shapes=[pltpu.VMEM((tm, d), BF16)],
        compiler_params=_params(("parallel", "arbitrary"), nbytes),
        name=name,
    )(*args)


def _res_matmul_kernel(*refs, n_a):
    a_refs = refs[:n_a]
    w_refs = refs[n_a:2 * n_a]
    res_ref, o_ref = refs[2 * n_a:]
    acc = res_ref[...]
    for a_ref, w_ref in zip(a_refs, w_refs):
        acc = acc + _dot(a_ref[...], w_ref[...])
    o_ref[...] = acc


def _res_matmul(a_list, w, res2d, *, tm, tn, name):
    n_a = len(a_list)
    n_rows, ka = a_list[0].shape
    assert n_a * ka == w.shape[0]
    n = w.shape[1]
    in_specs = [pl.BlockSpec((tm, ka), lambda i, j: (i, 0)) for _ in a_list]
    in_specs += [pl.BlockSpec((ka, tn), lambda i, j, r=r: (r, j)) for r in range(n_a)]
    in_specs += [pl.BlockSpec((tm, tn), lambda i, j: (i, j))]
    nbytes = 2 * tm * n_a * ka * 2 + 2 * n_a * ka * tn * 2 + 4 * tm * tn * 4 + tm * tn * 4
    return pl.pallas_call(
        functools.partial(_res_matmul_kernel, n_a=n_a),
        grid=(n_rows // tm, n // tn),
        in_specs=in_specs,
        out_specs=pl.BlockSpec((tm, tn), lambda i, j: (i, j)),
        out_shape=jax.ShapeDtypeStruct((n_rows, n), F32),
        compiler_params=_params(("parallel", "arbitrary"), nbytes),
        name=name,
    )(*a_list, *([w] * n_a), res2d)


CONV_T = 64
CONV_HALO = 32
CONV_TB = 16


def _conformer_kernel(val_ref, gate_ref, hval_ref, hgate_ref, wb_ref, cb_ref, lng_ref, lnb_ref, o_ref,
                      hx_ref, c_ref, *, taps):
    bsz, t_len, ch = val_ref.shape
    n_slab = ch // V7X_LANES
    keep = (pl.program_id(0) > 0).astype(F32)
    _to_time_major(hval_ref[...] * _sigmoid(hgate_ref[...]) * keep, hx_ref, 0)
    _to_time_major(val_ref[...] * _sigmoid(gate_ref[...]), hx_ref, CONV_HALO * bsz)

    lead = CONV_HALO - (taps - 1)
    for s in range(n_slab):
        def body(tb, carry, s=s):
            t0 = tb * CONV_TB
            acc = jnp.zeros((CONV_TB, bsz, V7X_LANES), F32)
            for k in range(taps):
                r0 = pl.multiple_of((t0 + lead + k) * bsz, bsz)
                rows = hx_ref[s, pl.ds(r0, CONV_TB * bsz), :].reshape(CONV_TB, bsz, V7X_LANES)
                acc = acc + wb_ref[k, :, s * V7X_LANES:(s + 1) * V7X_LANES][None] * rows
            c_ref[s, pl.ds(pl.multiple_of(t0 * bsz, bsz), CONV_TB * bsz), :] = acc.reshape(CONV_TB * bsz, V7X_LANES)
            return carry
        lax.fori_loop(0, t_len // CONV_TB, body, 0)

    h = jnp.concatenate([c_ref[s] for s in range(n_slab)], axis=-1) + cb_ref[...]
    mu = jnp.mean(h, axis=-1, keepdims=True)
    dlt = h - mu
    var = jnp.mean(dlt * dlt, axis=-1, keepdims=True)
    y = dlt * lax.rsqrt(var + EPS) * lng_ref[...] + lnb_ref[...]
    y = y * _sigmoid(y)
    for s in range(n_slab):
        c_ref[s] = y[:, s * V7X_LANES:(s + 1) * V7X_LANES]
    _from_time_major(c_ref, o_ref)


def _conformer(u3, conv_w, conv_b, ln_g, ln_b):
    bsz, seq, _ = u3.shape
    taps, ch = conv_w.shape
    assert taps - 1 <= CONV_HALO and CONV_T % CONV_HALO == 0 and seq % CONV_T == 0 and ch % V7X_LANES == 0
    wb = jnp.broadcast_to(conv_w[:, None, :], (taps, bsz, ch))
    ratio = CONV_T // CONV_HALO
    halo_idx = lambda c: (lambda t: (0, jnp.maximum(t * ratio - 1, 0), c))
    row = lambda a: a.reshape(1, ch)
    full2 = lambda t: (0, 0)
    nbytes = (2 * 2 * (CONV_T + CONV_HALO) * bsz * ch * 4 + 2 * taps * bsz * ch * 4
              + (2 * CONV_T + CONV_HALO) * bsz * ch * 4 + 2 * CONV_T * bsz * ch * 2 + 6 * CONV_T * bsz * ch * 4)
    return pl.pallas_call(
        functools.partial(_conformer_kernel, taps=taps),
        grid=(seq // CONV_T,),
        in_specs=[
            pl.BlockSpec((bsz, CONV_T, ch), lambda t: (0, t, 0)),
            pl.BlockSpec((bsz, CONV_T, ch), lambda t: (0, t, 1)),
            pl.BlockSpec((bsz, CONV_HALO, ch), halo_idx(0)),
            pl.BlockSpec((bsz, CONV_HALO, ch), halo_idx(1)),
            pl.BlockSpec((taps, bsz, ch), lambda t: (0, 0, 0)),
            pl.BlockSpec((1, ch), full2),
            pl.BlockSpec((1, ch), full2),
            pl.BlockSpec((1, ch), full2),
        ],
        out_specs=pl.BlockSpec((bsz, CONV_T, ch), lambda t: (0, t, 0)),
        out_shape=jax.ShapeDtypeStruct((bsz, seq, ch), BF16),
        scratch_shapes=[pltpu.VMEM((ch // V7X_LANES, (CONV_HALO + CONV_T) * bsz, V7X_LANES), F32),
                        pltpu.VMEM((ch // V7X_LANES, CONV_T * bsz, V7X_LANES), F32)],
        compiler_params=_params(("parallel",), nbytes),
        name="conformer",
    )(u3, u3, u3, u3, wb, row(conv_b), row(ln_g), row(ln_b))


S5_T = 32


def _s5_kernel(u_ref, wb_ref, lbar_ref, wcr_ref, wci_ref, d_ref, wg_ref, bg_ref, o_ref, ut_ref, bu_ref, st_ref):
    bsz, t_len, ch = u_ref.shape
    rows = t_len * bsz
    n_slabs, _, width = lbar_ref.shape
    half = width // 2
    cin = wb_ref.shape[1]

    @pl.when(pl.program_id(0) == 0)
    def _():
        st_ref[...] = jnp.zeros(st_ref.shape, F32)

    _to_time_major(u_ref[...], ut_ref, 0)
    u = jnp.concatenate([ut_ref[s] for s in range(ch // V7X_LANES)], axis=-1)
    ub = u.astype(BF16)
    for j in range(n_slabs):
        bu_ref[j] = _dot(ub[:, j * cin:(j + 1) * cin], wb_ref[j])

    ys = []
    for j in range(n_slabs):
        a = lbar_ref[j]
        ar, ai = a[:, :half], a[:, half:]
        s = st_ref[j]
        xr, xi = s[:, :half], s[:, half:]
        for t in range(t_len):
            b = bu_ref[j, t * bsz:(t + 1) * bsz, :]
            xr, xi = ar * xr - ai * xi + b[:, :half], ar * xi + ai * xr + b[:, half:]
            bu_ref[j, t * bsz:(t + 1) * bsz, :] = jnp.concatenate([xr, xi], axis=-1)
        st_ref[j] = jnp.concatenate([xr, xi], axis=-1)
        s = bu_ref[j]
        ys.append(_dot(s[:, :half].astype(BF16), wcr_ref[j]) - _dot(s[:, half:].astype(BF16), wci_ref[j]))
    y = jnp.concatenate(ys, axis=-1) + d_ref[...] * u
    y = jax.nn.gelu(y)
    gate = _sigmoid(_dot(y.astype(BF16), wg_ref[...]) + bg_ref[...])
    y = y * gate
    for s in range(ch // V7X_LANES):
        ut_ref[s] = y[:, s * V7X_LANES:(s + 1) * V7X_LANES]
    _from_time_major(ut_ref, o_ref)


def _s5(u3, cblk, lbar_re, lbar_im, bb_re, bb_im, c_re, c_im, d_skip, w_glu, b_glu):
    bsz, seq, _ = u3.shape
    g, h, p = bb_re.shape
    ch = g * h
    gs = g // N_SLABS
    half = gs * p
    eye = jnp.eye(gs, dtype=F32)

    def place_b(bb):
        return jnp.einsum("jghp,gk->jghkp", bb.reshape(N_SLABS, gs, h, p), eye).reshape(N_SLABS, gs * h, half)

    def place_c(c):
        return jnp.einsum("jghp,gk->jgpkh", c.reshape(N_SLABS, gs, h, p), eye).reshape(N_SLABS, half, gs * h)

    wb = jnp.concatenate([place_b(bb_re), place_b(bb_im)], axis=-1).astype(BF16)
    wcr = place_c(c_re).astype(BF16)
    wci = place_c(c_im).astype(BF16)
    lbar = jnp.concatenate([lbar_re.reshape(N_SLABS, half), lbar_im.reshape(N_SLABS, half)], axis=-1)
    lbar = jnp.broadcast_to(lbar[:, None, :], (N_SLABS, bsz, 2 * half))
    rows = S5_T * bsz
    full2 = lambda t: (0, 0)
    full3 = lambda t: (0, 0, 0)
    nbytes = (2 * rows * ch * 4 + 2 * rows * ch * 2 + rows * ch * 4 + N_SLABS * rows * 2 * half * 4
              + 2 * (wb.size + wcr.size + wci.size + w_glu.size) * 2 + 2 * lbar.size * 4 + 6 * rows * ch * 4)
    return pl.pallas_call(
        _s5_kernel,
        grid=(seq // S5_T,),
        in_specs=[
            pl.BlockSpec((bsz, S5_T, ch), lambda t: (0, t, cblk)),
            pl.BlockSpec(wb.shape, full3),
            pl.BlockSpec(lbar.shape, full3),
            pl.BlockSpec(wcr.shape, full3),
            pl.BlockSpec(wci.shape, full3),
            pl.BlockSpec((1, ch), full2),
            pl.BlockSpec(w_glu.shape, full2),
            pl.BlockSpec((1, ch), full2),
        ],
        out_specs=pl.BlockSpec((bsz, S5_T, ch), lambda t: (0, t, 0)),
        out_shape=jax.ShapeDtypeStruct((bsz, seq, ch), BF16),
        scratch_shapes=[pltpu.VMEM((ch // V7X_LANES, rows, V7X_LANES), F32),
                        pltpu.VMEM((N_SLABS, rows, 2 * half), F32),
                        pltpu.VMEM((N_SLABS, bsz, 2 * half), F32)],
        compiler_params=_params(("arbitrary",), nbytes),
        name="s5",
    )(u3, wb, lbar, wcr, wci, d_skip.reshape(1, ch), w_glu, b_glu.reshape(1, ch))


FFN_T = 64
FFN_TC = 512


def _ffn_kernel(x_ref, g_ref, wg_ref, wv_ref, cw_ref, wd_ref, *rest, taps, final):
    if final:
        fg_ref, o_ref, acc_ref, h_ref, carry_ref = rest
    else:
        o_ref, acc_ref, h_ref, carry_ref = rest
    bsz, t_len, d = x_ref.shape
    tm = bsz * t_len
    n_slab = d // V7X_LANES
    hist = (taps - 1) * bsz
    c = pl.program_id(1)

    def full_rows(ref):
        return jnp.concatenate([ref[s] for s in range(n_slab)], axis=-1)

    @pl.when(jnp.logical_and(pl.program_id(0) == 0, c == 0))
    def _():
        carry_ref[...] = jnp.zeros(carry_ref.shape, F32)

    @pl.when(c == 0)
    def _():
        _to_time_major(x_ref[...], acc_ref, 0)
        h_ref[...] = _rmsnorm(full_rows(acc_ref), g_ref[...]).astype(BF16)

    h = h_ref[...]

    def conv(u, cw, slot):
        ext = jnp.concatenate([carry_ref[c, slot], u], axis=0)
        carry_ref[c, slot] = u[tm - hist:tm]
        out = cw[0:1, :] * ext[0:tm]
        for k in range(1, taps):
            out = out + cw[k:k + 1, :] * ext[k * bsz:k * bsz + tm]
        return out

    n_c = pl.num_programs(1)
    gate = conv(_dot(h, wg_ref[...]), cw_ref[c], 0)
    val = conv(_dot(h, wv_ref[...]), cw_ref[n_c + c], 1)
    act = (gate * _sigmoid(gate) * val).astype(BF16)
    part = _dot(act, wd_ref[...])
    for s in range(n_slab):
        acc_ref[s] += part[:, s * V7X_LANES:(s + 1) * V7X_LANES]

    @pl.when(c == pl.num_programs(1) - 1)
    def _():
        if final:
            y = _rmsnorm(full_rows(acc_ref), fg_ref[...])
            for s in range(n_slab):
                acc_ref[s] = y[:, s * V7X_LANES:(s + 1) * V7X_LANES]
        _from_time_major(acc_ref, o_ref)


def _ffn(x3, gain, w_up, conv_w, w_down, layer, final_gain=None):
    bsz, seq, d = x3.shape
    taps = conv_w.shape[0]
    d_ff = w_down.shape[1]
    assert (taps - 1 <= FFN_T and seq % FFN_T == 0 and d_ff % FFN_TC == 0
            and bsz % V7X_SUBLANES == 0 and d % V7X_LANES == 0)
    nc = d_ff // FFN_TC
    tm = FFN_T * bsz
    hist = (taps - 1) * bsz
    final = final_gain is not None
    in_specs = [
        pl.BlockSpec((bsz, FFN_T, d), lambda i, c: (0, i, 0)),
        pl.BlockSpec((1, d), lambda i, c: (0, 0)),
        pl.BlockSpec((None, d, FFN_TC), lambda i, c: (layer, 0, c)),
        pl.BlockSpec((None, d, FFN_TC), lambda i, c: (layer, 0, nc + c)),
        pl.BlockSpec((2 * nc, taps, FFN_TC), lambda i, c: (0, 0, 0)),
        pl.BlockSpec((None, FFN_TC, d), lambda i, c: (layer, c, 0)),
    ]
    cw = jnp.transpose(conv_w.reshape(taps, 2 * nc, FFN_TC), (1, 0, 2))
    args = [x3, gain.reshape(1, d), w_up, w_up, cw, w_down]
    if final:
        in_specs.append(pl.BlockSpec((1, d), lambda i, c: (0, 0)))
        args.append(final_gain.reshape(1, d))
    nbytes = (5 * tm * d * 4 + tm * d * 2 + 2 * 3 * d * FFN_TC * 2 + nc * 2 * hist * FFN_TC * 4
              + 6 * (tm + hist) * FFN_TC * 4)
    return pl.pallas_call(
        functools.partial(_ffn_kernel, taps=taps, final=final),
        grid=(seq // FFN_T, nc),
        in_specs=in_specs,
        out_specs=pl.BlockSpec((bsz, FFN_T, d), lambda i, c: (0, i, 0)),
        out_shape=jax.ShapeDtypeStruct((bsz, seq, d), F32),
        scratch_shapes=[pltpu.VMEM((d // V7X_LANES, tm, V7X_LANES), F32),
                        pltpu.VMEM((tm, d), BF16),
                        pltpu.VMEM((nc, 2, hist, FFN_TC), F32)],
        compiler_params=_params(("arbitrary", "arbitrary"), nbytes),
        name="ffn",
    )(*args)


ATT_T = 256
ATT_G = 4
LOG2_E = 1.4426950408889634
F32_EXP2_UNDERFLOW = -150.0


def _attn_kernel(q_ref, k_ref, v_ref, tri_ref, o_ref, acc_ref, carry_ref):
    blk = tri_ref.shape[0]
    n_g = acc_ref.shape[0]
    dh = acc_ref.shape[1]
    n_q = q_ref.shape[0] // blk
    tri = tri_ref[...]

    def rows(kb):
        return pl.ds(pl.multiple_of(kb * blk, blk), blk)

    def process(qi, chains):
        cols = lambda g: slice(g * dh, (g + 1) * dh)
        zs = [lax.dot_general(k_ref[rows(kb), cols(g)], q_ref[rows(qi), cols(g)], (((1,), (1,)), ((), ())),
                              preferred_element_type=F32) for g, kb, _ in chains]
        before = (lax.broadcasted_iota(jnp.int32, (blk, blk), 0) < lax.broadcasted_iota(jnp.int32, (blk, blk), 1))
        parts = []
        for (_, _, diagonal), z in zip(chains, zs):
            log_beta = jnp.minimum(z, 0.0) - jnp.log(1.0 + jnp.exp2(-jnp.abs(z))) * LOG2_E
            log_keep = log_beta - z
            if diagonal:
                log_keep = jnp.where(before, log_keep, 0.0)
            parts.append((log_beta, log_keep))
        sufs = [_dot(tri, log_keep.astype(BF16)) for _, log_keep in parts]
        pvs = []
        for (g, kb, diagonal), (log_beta, _), suffix in zip(chains, parts, sufs):
            w = jnp.exp2(log_beta + suffix)
            if diagonal:
                w = jnp.where(before, w, 0.0)
            pvs.append(lax.dot_general(v_ref[rows(kb), cols(g)], w.astype(BF16), (((0,), (0,)), ((), ())),
                                       preferred_element_type=F32))
        for (g, _, _), (_, log_keep), pv in zip(chains, parts, pvs):
            carry = carry_ref[g]
            acc_ref[g] += jnp.exp2(carry) * pv
            carry_ref[g] = carry + jnp.sum(log_keep, axis=0, keepdims=True)

    def reset():
        acc_ref[...] = jnp.zeros(acc_ref.shape, F32)
        carry_ref[...] = jnp.zeros(carry_ref.shape, F32)

    def emit(qi):
        for g in range(n_g):
            o_ref[rows(qi), g * dh:(g + 1) * dh] = acc_ref[g].T.astype(o_ref.dtype)

    reset()
    process(0, [(g, 0, True) for g in range(n_g)])
    emit(0)

    def query_block(qi, state):
        reset()
        process(qi, [(g, qi, True) for g in range(n_g)] + [(g, qi - 1, False) for g in range(n_g)])

        def cond(inner):
            kb, max_carry = inner
            return jnp.logical_and(kb >= 0, max_carry > F32_EXP2_UNDERFLOW)

        def body(inner):
            kb, _ = inner
            process(qi, [(g, kb, False) for g in range(n_g)])
            return kb - 1, jnp.max(carry_ref[...])

        lax.while_loop(cond, body, (qi - 2, jnp.max(carry_ref[...])))
        emit(qi)
        return state
    lax.fori_loop(1, n_q, query_block, 0)


def _attention(qkv, bsz, seq):
    d = qkv.shape[1] // 3
    dh = d // N_HEADS
    assert seq % ATT_T == 0 and N_HEADS % ATT_G == 0
    hg = N_HEADS // ATT_G
    gw = ATT_G * dh
    idx = jnp.arange(ATT_T)
    tri = (idx[None, :] > idx[:, None]).astype(BF16)
    nbytes = (2 * 4 * seq * gw * 2 + 2 * ATT_T * ATT_T * 2
              + ATT_G * (dh + V7X_SUBLANES) * ATT_T * 4 + 2 * ATT_G * 10 * ATT_T * ATT_T * 4)
    return pl.pallas_call(
        _attn_kernel,
        grid=(bsz, hg),
        in_specs=[
            pl.BlockSpec((seq, gw), lambda b, h: (b, h)),
            pl.BlockSpec((seq, gw), lambda b, h: (b, hg + h)),
            pl.BlockSpec((seq, gw), lambda b, h: (b, 2 * hg + h)),
            pl.BlockSpec((ATT_T, ATT_T), lambda b, h: (0, 0)),
        ],
        out_specs=pl.BlockSpec((seq, gw), lambda b, h: (b, h)),
        out_shape=jax.ShapeDtypeStruct((bsz * seq, d), BF16),
        scratch_shapes=[pltpu.VMEM((ATT_G, dh, ATT_T), F32), pltpu.VMEM((ATT_G, 1, ATT_T), F32)],
        compiler_params=_params(("parallel", "parallel"), nbytes),
        name="attention",
    )(qkv, qkv, qkv, tri)


PROJ_TM = 1024
PROJ_TN = 1024


def kernel(x, ln_mix_even, w_in, conv_w, conv_b, conv_ln_g, conv_ln_b, ssm_lam_re, ssm_lam_im, ssm_log_step, ssm_b_re, ssm_b_im, ssm_c_re, ssm_c_im, ssm_d, ssm_w_glu, ssm_b_glu, w_out_even, ln_mix_odd, w_qkv, w_o, ln_ffn, ffn_w_up, ffn_conv_w, ffn_w_down, ln_final):
    bsz, seq, d = x.shape
    depth = ln_ffn.shape[0]
    assert depth == 2 and ln_mix_even.shape[0] == 1 and ln_mix_odd.shape[0] == 1
    tm = min(PROJ_TM, seq)
    tn = min(PROJ_TN, d)
    d_conv = conv_w.shape[-1]
    d_ssm = ssm_w_glu.shape[-1]
    d_in = w_in.shape[-1]
    assert d_conv == d_ssm and d_in == 2 * d_conv + d_ssm
    x2d = x.reshape(bsz * seq, d)
    w_up_bf = _to_bf16(ffn_w_up)
    w_down_bf = _to_bf16(ffn_w_down)
    single = lambda w: _to_bf16(w).reshape(w.shape[1:])

    u = _norm_matmul(x2d, ln_mix_even[0], single(w_in), F32, tm=tm, tn=tn, name="in_proj")
    u3 = u.reshape(bsz, seq, d_in)
    ya = _conformer(u3, conv_w[0], conv_b[0], conv_ln_g[0], conv_ln_b[0])
    lbar_re, lbar_im, bb_re, bb_im = _zoh(ssm_lam_re[0], ssm_lam_im[0], ssm_log_step[0], ssm_b_re[0], ssm_b_im[0])
    yb = _s5(u3, 2, lbar_re, lbar_im, bb_re, bb_im, ssm_c_re[0], ssm_c_im[0], ssm_d[0],
             single(ssm_w_glu), ssm_b_glu[0])
    x2d = _res_matmul([ya.reshape(bsz * seq, d_conv), yb.reshape(bsz * seq, d_ssm)], single(w_out_even),
                      x2d, tm=tm, tn=tn, name="out_proj")
    x2d = _ffn(x2d.reshape(bsz, seq, d), ln_ffn[0], w_up_bf, ffn_conv_w[0], w_down_bf, 0).reshape(bsz * seq, d)

    dh = d // N_HEADS
    col_scale = jnp.concatenate([jnp.full((d,), dh ** -0.5 * LOG2_E, F32), jnp.ones((2 * d,), F32)])
    qkv = _norm_matmul(x2d, ln_mix_odd[0], single(w_qkv), BF16, tm=tm, tn=tn, col_scale=col_scale,
                       name="qkv_proj")
    o = _attention(qkv, bsz, seq)
    x2d = _res_matmul([o], single(w_o), x2d, tm=tm, tn=tn, name="o_proj")
    return _ffn(x2d.reshape(bsz, seq, d), ln_ffn[1], w_up_bf, ffn_conv_w[1], w_down_bf, 1, final_gain=ln_final)
```

```python
import functools

import jax
import jax.numpy as jnp
from jax import lax
from jax.experimental import pallas as pl
from jax.experimental.pallas import tpu as pltpu

F32 = jnp.float32
BF16 = jnp.bfloat16

EPS = 1e-6

V7X_SUBLANES = 8
V7X_LANES = 128
V7X_VMEM_BYTES = 64 * 1024 * 1024

N_HEADS = 16
N_SLABS = 8


def _vmem_limit(nbytes):
    return int(min(max(nbytes * 3 // 2, 16 * 1024 * 1024), V7X_VMEM_BYTES * 7 // 8))


def _params(semantics, nbytes):
    return pltpu.CompilerParams(dimension_semantics=semantics, vmem_limit_bytes=_vmem_limit(nbytes))


def _dot(a, b):
    return jnp.dot(a, b, preferred_element_type=F32)


def _rmsnorm(x, g):
    ms = jnp.mean(x * x, axis=-1, keepdims=True)
    return x * lax.rsqrt(ms + EPS) * g


def _sigmoid(x):
    return 1.0 / (1.0 + jnp.exp(-x))


def _to_time_major(src, dst_ref, row0):
    bsz, t_len, ch = src.shape
    for b in range(bsz):
        for s in range(ch // V7X_LANES):
            dst_ref[s, pl.ds(row0 + b, t_len, stride=bsz), :] = src[b, :, s * V7X_LANES:(s + 1) * V7X_LANES]


def _from_time_major(src_ref, o_ref):
    bsz, t_len, ch = o_ref.shape
    for b in range(bsz):
        for s in range(ch // V7X_LANES):
            o_ref[b, :, s * V7X_LANES:(s + 1) * V7X_LANES] = (
                src_ref[s, pl.ds(b, t_len, stride=bsz), :].astype(o_ref.dtype))


CAST_TR = 512
CAST_TC_MAX = 4096


def _cast_kernel(x_ref, o_ref):
    o_ref[...] = x_ref[...].astype(o_ref.dtype)


def _to_bf16(w3):
    n, r, c = w3.shape
    tr = min(r, CAST_TR)
    tc = max(t for t in range(V7X_LANES, min(c, CAST_TC_MAX) + 1, V7X_LANES) if c % t == 0)
    assert r % tr == 0
    spec = pl.BlockSpec((None, tr, tc), lambda l, i, j: (l, i, j))
    return pl.pallas_call(
        _cast_kernel,
        grid=(n, r // tr, c // tc),
        in_specs=[spec],
        out_specs=spec,
        out_shape=jax.ShapeDtypeStruct(w3.shape, BF16),
        compiler_params=_params(("parallel", "parallel", "parallel"), 2 * tr * tc * 6),
        name="cast",
    )(w3)


def _zoh_kernel(lr_ref, li_ref, ls_ref, br_ref, bi_ref, are_ref, aim_ref, bbr_ref, bbi_ref):
    lr = lr_ref[...]
    li = li_ref[...]
    step = jnp.exp(ls_ref[...])
    mag = jnp.exp(lr * step)
    lbar_re = mag * jnp.cos(li * step)
    lbar_im = mag * jnp.sin(li * step)
    num_re = lbar_re - 1.0
    num_im = lbar_im
    den = lr * lr + li * li
    f_re = (num_re * lr + num_im * li) / den
    f_im = (num_im * lr - num_re * li) / den
    br = br_ref[...]
    bi = bi_ref[...]
    are_ref[...] = lbar_re
    aim_ref[...] = lbar_im
    bbr_ref[...] = f_re * br - f_im * bi
    bbi_ref[...] = f_re * bi + f_im * br


def _zoh(lam_re, lam_im, log_step, b_re, b_im):
    g, p = lam_re.shape
    h = b_re.shape[-1]
    rep = lambda a: jnp.broadcast_to(a[:, None, :], (g, h, p)).reshape(g * h, p)
    args = (rep(lam_re), rep(lam_im), rep(jnp.broadcast_to(log_step[:, None], (g, p))),
            jnp.transpose(b_re, (0, 2, 1)).reshape(g * h, p),
            jnp.transpose(b_im, (0, 2, 1)).reshape(g * h, p))
    out = pl.pallas_call(
        _zoh_kernel,
        out_shape=[jax.ShapeDtypeStruct((g * h, p), F32)] * 4,
        name="zoh",
    )(*args)
    are, aim, bbr, bbi = [o.reshape(g, h, p) for o in out]
    return are[:, 0], aim[:, 0], bbr, bbi


def _norm_matmul_kernel(x_ref, g_ref, w_ref, *rest, has_scale):
    if has_scale:
        s_ref, o_ref, h_ref = rest
    else:
        o_ref, h_ref = rest

    @pl.when(pl.program_id(1) == 0)
    def _():
        h_ref[...] = _rmsnorm(x_ref[...], g_ref[...]).astype(BF16)

    acc = _dot(h_ref[...], w_ref[...])
    if has_scale:
        acc = acc * s_ref[...]
    o_ref[...] = acc.astype(o_ref.dtype)


def _norm_matmul(x2d, gain, w, out_dtype, *, tm, tn, col_scale=None, name):
    n_rows = x2d.shape[0]
    d, n = w.shape
    in_specs = [
        pl.BlockSpec((tm, d), lambda i, j: (i, 0)),
        pl.BlockSpec((1, d), lambda i, j: (0, 0)),
        pl.BlockSpec((d, tn), lambda i, j: (0, j)),
    ]
    args = [x2d, gain.reshape(1, d), w]
    if col_scale is not None:
        in_specs.append(pl.BlockSpec((1, tn), lambda i, j: (0, j)))
        args.append(col_scale.reshape(1, n))
    out_bytes = jnp.dtype(out_dtype).itemsize
    nbytes = 2 * tm * d * 4 + tm * d * 2 + 2 * d * tn * 2 + 2 * tm * tn * out_bytes + tm * tn * 4
    return pl.pallas_call(
        functools.partial(_norm_matmul_kernel, has_scale=col_scale is not None),
        grid=(n_rows // tm, n // tn),
        in_specs=in_specs,
        out_specs=pl.BlockSpec((tm, tn), lambda i, j: (i, j)),
        out_shape=jax.ShapeDtypeStruct((n_rows, n), out_dtype),
        scratch_shapes=[pltpu.VMEM((tm, d), BF16)],
        compiler_params=_params(("parallel", "arbitrary"), nbytes),
        name=name,
    )(*args)


def _res_matmul_kernel(*refs, n_a):
    a_refs = refs[:n_a]
    w_refs = refs[n_a:2 * n_a]
    res_ref, o_ref = refs[2 * n_a:]
    acc = res_ref[...]
    for a_ref, w_ref in zip(a_refs, w_refs):
        acc = acc + _dot(a_ref[...], w_ref[...])
    o_ref[...] = acc


def _res_matmul(a_list, w, res2d, *, tm, tn, name):
    n_a = len(a_list)
    n_rows, ka = a_list[0].shape
    assert n_a * ka == w.shape[0]
    n = w.shape[1]
    in_specs = [pl.BlockSpec((tm, ka), lambda i, j: (i, 0)) for _ in a_list]
    in_specs += [pl.BlockSpec((ka, tn), lambda i, j, r=r: (r, j)) for r in range(n_a)]
    in_specs += [pl.BlockSpec((tm, tn), lambda i, j: (i, j))]
    nbytes = 2 * tm * n_a * ka * 2 + 2 * n_a * ka * tn * 2 + 4 * tm * tn * 4 + tm * tn * 4
    return pl.pallas_call(
        functools.partial(_res_matmul_kernel, n_a=n_a),
        grid=(n_rows // tm, n // tn),
        in_specs=in_specs,
        out_specs=pl.BlockSpec((tm, tn), lambda i, j: (i, j)),
        out_shape=jax.ShapeDtypeStruct((n_rows, n), F32),
        compiler_params=_params(("parallel", "arbitrary"), nbytes),
        name=name,
    )(*a_list, *([w] * n_a), res2d)


CONV_T = 128
CONV_HALO = 32
CONV_TB = 16


def _conformer_kernel(val_ref, gate_ref, hval_ref, hgate_ref, wb_ref, cb_ref, lng_ref, lnb_ref, o_ref,
                      hx_ref, c_ref, *, taps):
    bsz, t_len, ch = val_ref.shape
    n_slab = ch // V7X_LANES
    keep = (pl.program_id(0) > 0).astype(F32)
    _to_time_major(hval_ref[...] * _sigmoid(hgate_ref[...]) * keep, hx_ref, 0)
    _to_time_major(val_ref[...] * _sigmoid(gate_ref[...]), hx_ref, CONV_HALO * bsz)

    lead = CONV_HALO - (taps - 1)
    for s in range(n_slab):
        def body(tb, carry, s=s):
            t0 = tb * CONV_TB
            acc = jnp.zeros((CONV_TB, bsz, V7X_LANES), F32)
            for k in range(taps):
                r0 = pl.multiple_of((t0 + lead + k) * bsz, bsz)
                rows = hx_ref[s, pl.ds(r0, CONV_TB * bsz), :].reshape(CONV_TB, bsz, V7X_LANES)
                acc = acc + wb_ref[k, :, s * V7X_LANES:(s + 1) * V7X_LANES][None] * rows
            c_ref[s, pl.ds(pl.multiple_of(t0 * bsz, bsz), CONV_TB * bsz), :] = acc.reshape(CONV_TB * bsz, V7X_LANES)
            return carry
        lax.fori_loop(0, t_len // CONV_TB, body, 0)

    h = jnp.concatenate([c_ref[s] for s in range(n_slab)], axis=-1) + cb_ref[...]
    mu = jnp.mean(h, axis=-1, keepdims=True)
    dlt = h - mu
    var = jnp.mean(dlt * dlt, axis=-1, keepdims=True)
    y = dlt * lax.rsqrt(var + EPS) * lng_ref[...] + lnb_ref[...]
    y = y * _sigmoid(y)
    for s in range(n_slab):
        c_ref[s] = y[:, s * V7X_LANES:(s + 1) * V7X_LANES]
    _from_time_major(c_ref, o_ref)


def _conformer(u3, conv_w, conv_b, ln_g, ln_b):
    bsz, seq, _ = u3.shape
    taps, ch = conv_w.shape
    assert taps - 1 <= CONV_HALO and CONV_T % CONV_HALO == 0 and seq % CONV_T == 0 and ch % V7X_LANES == 0
    wb = jnp.broadcast_to(conv_w[:, None, :], (taps, bsz, ch))
    ratio = CONV_T // CONV_HALO
    halo_idx = lambda c: (lambda t: (0, jnp.maximum(t * ratio - 1, 0), c))
    row = lambda a: a.reshape(1, ch)
    full2 = lambda t: (0, 0)
    nbytes = (2 * 2 * (CONV_T + CONV_HALO) * bsz * ch * 4 + 2 * taps * bsz * ch * 4
              + (2 * CONV_T + CONV_HALO) * bsz * ch * 4 + 2 * CONV_T * bsz * ch * 2 + 6 * CONV_T * bsz * ch * 4)
    return pl.pallas_call(
        functools.partial(_conformer_kernel, taps=taps),
        grid=(seq // CONV_T,),
        in_specs=[
            pl.BlockSpec((bsz, CONV_T, ch), lambda t: (0, t, 0)),
            pl.BlockSpec((bsz, CONV_T, ch), lambda t: (0, t, 1)),
            pl.BlockSpec((bsz, CONV_HALO, ch), halo_idx(0)),
            pl.BlockSpec((bsz, CONV_HALO, ch), halo_idx(1)),
            pl.BlockSpec((taps, bsz, ch), lambda t: (0, 0, 0)),
            pl.BlockSpec((1, ch), full2),
            pl.BlockSpec((1, ch), full2),
            pl.BlockSpec((1, ch), full2),
        ],
        out_specs=pl.BlockSpec((bsz, CONV_T, ch), lambda t: (0, t, 0)),
        out_shape=jax.ShapeDtypeStruct((bsz, seq, ch), BF16),
        scratch_shapes=[pltpu.VMEM((ch // V7X_LANES, (CONV_HALO + CONV_T) * bsz, V7X_LANES), F32),
                        pltpu.VMEM((ch // V7X_LANES, CONV_T * bsz, V7X_LANES), F32)],
        compiler_params=_params(("parallel",), nbytes),
        name="conformer",
    )(u3, u3, u3, u3, wb, row(conv_b), row(ln_g), row(ln_b))


S5_T = 32


def _s5_kernel(u_ref, wb_ref, lbar_ref, wcr_ref, wci_ref, d_ref, wg_ref, bg_ref, o_ref, ut_ref, bu_ref, st_ref):
    bsz, t_len, ch = u_ref.shape
    rows = t_len * bsz
    n_slabs, _, width = lbar_ref.shape
    half = width // 2
    cin = wb_ref.shape[1]

    @pl.when(pl.program_id(0) == 0)
    def _():
        st_ref[...] = jnp.zeros(st_ref.shape, F32)

    _to_time_major(u_ref[...], ut_ref, 0)
    u = jnp.concatenate([ut_ref[s] for s in range(ch // V7X_LANES)], axis=-1)
    ub = u.astype(BF16)
    for j in range(n_slabs):
        bu_ref[j] = _dot(ub[:, j * cin:(j + 1) * cin], wb_ref[j])

    ys = []
    for j in range(n_slabs):
        a = lbar_ref[j]
        ar, ai = a[:, :half], a[:, half:]
        s = st_ref[j]
        xr, xi = s[:, :half], s[:, half:]
        for t in range(t_len):
            b = bu_ref[j, t * bsz:(t + 1) * bsz, :]
            xr, xi = ar * xr - ai * xi + b[:, :half], ar * xi + ai * xr + b[:, half:]
            bu_ref[j, t * bsz:(t + 1) * bsz, :] = jnp.concatenate([xr, xi], axis=-1)
        st_ref[j] = jnp.concatenate([xr, xi], axis=-1)
        s = bu_ref[j]
        ys.append(_dot(s[:, :half].astype(BF16), wcr_ref[j]) - _dot(s[:, half:].astype(BF16), wci_ref[j]))
    y = jnp.concatenate(ys, axis=-1) + d_ref[...] * u
    y = jax.nn.gelu(y)
    gate = _sigmoid(_dot(y.astype(BF16), wg_ref[...]) + bg_ref[...])
    y = y * gate
    for s in range(ch // V7X_LANES):
        ut_ref[s] = y[:, s * V7X_LANES:(s + 1) * V7X_LANES]
    _from_time_major(ut_ref, o_ref)


def _s5(u3, cblk, lbar_re, lbar_im, bb_re, bb_im, c_re, c_im, d_skip, w_glu, b_glu):
    bsz, seq, _ = u3.shape
    g, h, p = bb_re.shape
    ch = g * h
    gs = g // N_SLABS
    half = gs * p
    eye = jnp.eye(gs, dtype=F32)

    def place_b(bb):
        return jnp.einsum("jghp,gk->jghkp", bb.reshape(N_SLABS, gs, h, p), eye).reshape(N_SLABS, gs * h, half)

    def place_c(c):
        return jnp.einsum("jghp,gk->jgpkh", c.reshape(N_SLABS, gs, h, p), eye).reshape(N_SLABS, half, gs * h)

    wb = jnp.concatenate([place_b(bb_re), place_b(bb_im)], axis=-1).astype(BF16)
    wcr = place_c(c_re).astype(BF16)
    wci = place_c(c_im).astype(BF16)
    lbar = jnp.concatenate([lbar_re.reshape(N_SLABS, half), lbar_im.reshape(N_SLABS, half)], axis=-1)
    lbar = jnp.broadcast_to(lbar[:, None, :], (N_SLABS, bsz, 2 * half))
    rows = S5_T * bsz
    full2 = lambda t: (0, 0)
    full3 = lambda t: (0, 0, 0)
    nbytes = (2 * rows * ch * 4 + 2 * rows * ch * 2 + rows * ch * 4 + N_SLABS * rows * 2 * half * 4
              + 2 * (wb.size + wcr.size + wci.size + w_glu.size) * 2 + 2 * lbar.size * 4 + 6 * rows * ch * 4)
    return pl.pallas_call(
        _s5_kernel,
        grid=(seq // S5_T,),
        in_specs=[
            pl.BlockSpec((bsz, S5_T, ch), lambda t: (0, t, cblk)),
            pl.BlockSpec(wb.shape, full3),
            pl.BlockSpec(lbar.shape, full3),
            pl.BlockSpec(wcr.shape, full3),
            pl.BlockSpec(wci.shape, full3),
            pl.BlockSpec((1, ch), full2),
            pl.BlockSpec(w_glu.shape, full2),
            pl.BlockSpec((1, ch), full2),
        ],
        out_specs=pl.BlockSpec((bsz, S5_T, ch), lambda t: (0, t, 0)),
        out_shape=jax.ShapeDtypeStruct((bsz, seq, ch), BF16),
        scratch_shapes=[pltpu.VMEM((ch // V7X_LANES, rows, V7X_LANES), F32),
                        pltpu.VMEM((N_SLABS, rows, 2 * half), F32),
                        pltpu.VMEM((N_SLABS, bsz, 2 * half), F32)],
        compiler_params=_params(("arbitrary",), nbytes),
        name="s5",
    )(u3, wb, lbar, wcr, wci, d_skip.reshape(1, ch), w_glu, b_glu.reshape(1, ch))


FFN_T = 64
FFN_TC = 512


def _ffn_kernel(x_ref, g_ref, wg_ref, wv_ref, cw_ref, wd_ref, *rest, taps, final):
    if final:
        fg_ref, o_ref, acc_ref, h_ref, carry_ref = rest
    else:
        o_ref, acc_ref, h_ref, carry_ref = rest
    bsz, t_len, d = x_ref.shape
    tm = bsz * t_len
    n_slab = d // V7X_LANES
    hist = (taps - 1) * bsz
    c = pl.program_id(1)

    def full_rows(ref):
        return jnp.concatenate([ref[s] for s in range(n_slab)], axis=-1)

    @pl.when(jnp.logical_and(pl.program_id(0) == 0, c == 0))
    def _():
        carry_ref[...] = jnp.zeros(carry_ref.shape, F32)

    @pl.when(c == 0)
    def _():
        _to_time_major(x_ref[...], acc_ref, 0)
        h_ref[...] = _rmsnorm(full_rows(acc_ref), g_ref[...]).astype(BF16)

    h = h_ref[...]

    def conv(u, cw, slot):
        ext = jnp.concatenate([carry_ref[c, slot], u], axis=0)
        carry_ref[c, slot] = u[tm - hist:tm]
        out = cw[0:1, :] * ext[0:tm]
        for k in range(1, taps):
            out = out + cw[k:k + 1, :] * ext[k * bsz:k * bsz + tm]
        return out

    n_c = pl.num_programs(1)
    gate = conv(_dot(h, wg_ref[...]), cw_ref[c], 0)
    val = conv(_dot(h, wv_ref[...]), cw_ref[n_c + c], 1)
    act = (gate * _sigmoid(gate) * val).astype(BF16)
    part = _dot(act, wd_ref[...])
    for s in range(n_slab):
        acc_ref[s] += part[:, s * V7X_LANES:(s + 1) * V7X_LANES]

    @pl.when(c == pl.num_programs(1) - 1)
    def _():
        if final:
            y = _rmsnorm(full_rows(acc_ref), fg_ref[...])
            for s in range(n_slab):
                acc_ref[s] = y[:, s * V7X_LANES:(s + 1) * V7X_LANES]
        _from_time_major(acc_ref, o_ref)


def _ffn(x3, gain, w_up, conv_w, w_down, layer, final_gain=None):
    bsz, seq, d = x3.shape
    taps = conv_w.shape[0]
    d_ff = w_down.shape[1]
    assert (taps - 1 <= FFN_T and seq % FFN_T == 0 and d_ff % FFN_TC == 0
            and bsz % V7X_SUBLANES == 0 and d % V7X_LANES == 0)
    nc = d_ff // FFN_TC
    tm = FFN_T * bsz
    hist = (taps - 1) * bsz
    final = final_gain is not None
    in_specs = [
        pl.BlockSpec((bsz, FFN_T, d), lambda i, c: (0, i, 0)),
        pl.BlockSpec((1, d), lambda i, c: (0, 0)),
        pl.BlockSpec((None, d, FFN_TC), lambda i, c: (layer, 0, c)),
        pl.BlockSpec((None, d, FFN_TC), lambda i, c: (layer, 0, nc + c)),
        pl.BlockSpec((2 * nc, taps, FFN_TC), lambda i, c: (0, 0, 0)),
        pl.BlockSpec((None, FFN_TC, d), lambda i, c: (layer, c, 0)),
    ]
    cw = jnp.transpose(conv_w.reshape(taps, 2 * nc, FFN_TC), (1, 0, 2))
    args = [x3, gain.reshape(1, d), w_up, w_up, cw, w_down]
    if final:
        in_specs.append(pl.BlockSpec((1, d), lambda i, c: (0, 0)))
        args.append(final_gain.reshape(1, d))
    nbytes = (5 * tm * d * 4 + tm * d * 2 + 2 * 3 * d * FFN_TC * 2 + nc * 2 * hist * FFN_TC * 4
              + 6 * (tm + hist) * FFN_TC * 4)
    return pl.pallas_call(
        functools.partial(_ffn_kernel, taps=taps, final=final),
        grid=(seq // FFN_T, nc),
        in_specs=in_specs,
        out_specs=pl.BlockSpec((bsz, FFN_T, d), lambda i, c: (0, i, 0)),
        out_shape=jax.ShapeDtypeStruct((bsz, seq, d), F32),
        scratch_shapes=[pltpu.VMEM((d // V7X_LANES, tm, V7X_LANES), F32),
                        pltpu.VMEM((tm, d), BF16),
                        pltpu.VMEM((nc, 2, hist, FFN_TC), F32)],
        compiler_params=_params(("arbitrary", "arbitrary"), nbytes),
        name="ffn",
    )(*args)


ATT_T = 256
ATT_G = 4
LOG2_E = 1.4426950408889634
F32_EXP2_UNDERFLOW = -150.0


def _attn_kernel(q_ref, k_ref, v_ref, tri_ref, o_ref, acc_ref, carry_ref):
    blk = tri_ref.shape[0]
    n_g = acc_ref.shape[0]
    dh = acc_ref.shape[1]
    n_q = q_ref.shape[0] // blk
    tri = tri_ref[...]

    def rows(kb):
        return pl.ds(pl.multiple_of(kb * blk, blk), blk)

    def process(qi, chains):
        cols = lambda g: slice(g * dh, (g + 1) * dh)
        zs = [lax.dot_general(k_ref[rows(kb), cols(g)], q_ref[rows(qi), cols(g)], (((1,), (1,)), ((), ())),
                              preferred_element_type=F32) for g, kb, _ in chains]
        before = (lax.broadcasted_iota(jnp.int32, (blk, blk), 0) < lax.broadcasted_iota(jnp.int32, (blk, blk), 1))
        parts = []
        for (_, _, diagonal), z in zip(chains, zs):
            log_beta = jnp.minimum(z, 0.0) - jnp.log(1.0 + jnp.exp2(-jnp.abs(z))) * LOG2_E
            log_keep = log_beta - z
            if diagonal:
                log_keep = jnp.where(before, log_keep, 0.0)
            parts.append((log_beta, log_keep))
        sufs = [_dot(tri, log_keep.astype(BF16)) for _, log_keep in parts]
        pvs = []
        for (g, kb, diagonal), (log_beta, _), suffix in zip(chains, parts, sufs):
            w = jnp.exp2(log_beta + suffix)
            if diagonal:
                w = jnp.where(before, w, 0.0)
            pvs.append(lax.dot_general(v_ref[rows(kb), cols(g)], w.astype(BF16), (((0,), (0,)), ((), ())),
                                       preferred_element_type=F32))
        for (g, _, _), (_, log_keep), pv in zip(chains, parts, pvs):
            carry = carry_ref[g]
            acc_ref[g] += jnp.exp2(carry) * pv
            carry_ref[g] = carry + jnp.sum(log_keep, axis=0, keepdims=True)

    def reset():
        acc_ref[...] = jnp.zeros(acc_ref.shape, F32)
        carry_ref[...] = jnp.zeros(carry_ref.shape, F32)

    def emit(qi):
        for g in range(n_g):
            o_ref[rows(qi), g * dh:(g + 1) * dh] = acc_ref[g].T.astype(o_ref.dtype)

    reset()
    process(0, [(g, 0, True) for g in range(n_g)])
    emit(0)

    def query_block(qi, state):
        reset()
        process(qi, [(g, qi, True) for g in range(n_g)] + [(g, qi - 1, False) for g in range(n_g)])

        def cond(inner):
            kb, max_carry = inner
            return jnp.logical_and(kb >= 0, max_carry > F32_EXP2_UNDERFLOW)

        def body(inner):
            kb, _ = inner
            process(qi, [(g, kb, False) for g in range(n_g)])
            return kb - 1, jnp.max(carry_ref[...])

        lax.while_loop(cond, body, (qi - 2, jnp.max(carry_ref[...])))
        emit(qi)
        return state
    lax.fori_loop(1, n_q, query_block, 0)


def _attention(qkv, bsz, seq):
    d = qkv.shape[1] // 3
    dh = d // N_HEADS
    assert seq % ATT_T == 0 and N_HEADS % ATT_G == 0
    hg = N_HEADS // ATT_G
    gw = ATT_G * dh
    idx = jnp.arange(ATT_T)
    tri = (idx[None, :] > idx[:, None]).astype(BF16)
    nbytes = (2 * 4 * seq * gw * 2 + 2 * ATT_T * ATT_T * 2
              + ATT_G * (dh + V7X_SUBLANES) * ATT_T * 4 + 2 * ATT_G * 10 * ATT_T * ATT_T * 4)
    return pl.pallas_call(
        _attn_kernel,
        grid=(bsz, hg),
        in_specs=[
            pl.BlockSpec((seq, gw), lambda b, h: (b, h)),
            pl.BlockSpec((seq, gw), lambda b, h: (b, hg + h)),
            pl.BlockSpec((seq, gw), lambda b, h: (b, 2 * hg + h)),
            pl.BlockSpec((ATT_T, ATT_T), lambda b, h: (0, 0)),
        ],
        out_specs=pl.BlockSpec((seq, gw), lambda b, h: (b, h)),
        out_shape=jax.ShapeDtypeStruct((bsz * seq, d), BF16),
        scratch_shapes=[pltpu.VMEM((ATT_G, dh, ATT_T), F32), pltpu.VMEM((ATT_G, 1, ATT_T), F32)],
        compiler_params=_params(("parallel", "parallel"), nbytes),
        name="attention",
    )(qkv, qkv, qkv, tri)


PROJ_TM = 1024
PROJ_TN = 1024


def kernel(x, ln_mix_even, w_in, conv_w, conv_b, conv_ln_g, conv_ln_b, ssm_lam_re, ssm_lam_im, ssm_log_step, ssm_b_re, ssm_b_im, ssm_c_re, ssm_c_im, ssm_d, ssm_w_glu, ssm_b_glu, w_out_even, ln_mix_odd, w_qkv, w_o, ln_ffn, ffn_w_up, ffn_conv_w, ffn_w_down, ln_final):
    bsz, seq, d = x.shape
    depth = ln_ffn.shape[0]
    assert depth == 2 and ln_mix_even.shape[0] == 1 and ln_mix_odd.shape[0] == 1
    tm = min(PROJ_TM, seq)
    tn = min(PROJ_TN, d)
    d_conv = conv_w.shape[-1]
    d_ssm = ssm_w_glu.shape[-1]
    d_in = w_in.shape[-1]
    assert d_conv == d_ssm and d_in == 2 * d_conv + d_ssm
    x2d = x.reshape(bsz * seq, d)
    w_up_bf = _to_bf16(ffn_w_up)
    w_down_bf = _to_bf16(ffn_w_down)
    single = lambda w: _to_bf16(w).reshape(w.shape[1:])

    u = _norm_matmul(x2d, ln_mix_even[0], single(w_in), F32, tm=tm, tn=tn, name="in_proj")
    u3 = u.reshape(bsz, seq, d_in)
    ya = _conformer(u3, conv_w[0], conv_b[0], conv_ln_g[0], conv_ln_b[0])
    lbar_re, lbar_im, bb_re, bb_im = _zoh(ssm_lam_re[0], ssm_lam_im[0], ssm_log_step[0], ssm_b_re[0], ssm_b_im[0])
    yb = _s5(u3, 2, lbar_re, lbar_im, bb_re, bb_im, ssm_c_re[0], ssm_c_im[0], ssm_d[0],
             single(ssm_w_glu), ssm_b_glu[0])
    x2d = _res_matmul([ya.reshape(bsz * seq, d_conv), yb.reshape(bsz * seq, d_ssm)], single(w_out_even),
                      x2d, tm=tm, tn=tn, name="out_proj")
    x2d = _ffn(x2d.reshape(bsz, seq, d), ln_ffn[0], w_up_bf, ffn_conv_w[0], w_down_bf, 0).reshape(bsz * seq, d)

    dh = d // N_HEADS
    col_scale = jnp.concatenate([jnp.full((d,), dh ** -0.5 * LOG2_E, F32), jnp.ones((2 * d,), F32)])
    qkv = _norm_matmul(x2d, ln_mix_odd[0], single(w_qkv), BF16, tm=tm, tn=tn, col_scale=col_scale,
                       name="qkv_proj")
    o = _attention(qkv, bsz, seq)
    x2d = _res_matmul([o], single(w_o), x2d, tm=tm, tn=tn, name="o_proj")
    return _ffn(x2d.reshape(bsz, seq, d), ln_ffn[1], w_up_bf, ffn_conv_w[1], w_down_bf, 1, final_gain=ln_final)
```

```python
import functools

import jax
import jax.numpy as jnp
from jax import lax
from jax.experimental import pallas as pl
from jax.experimental.pallas import tpu as pltpu

F32 = jnp.float32
BF16 = jnp.bfloat16

EPS = 1e-6

V7X_SUBLANES = 8
V7X_LANES = 128
V7X_VMEM_BYTES = 64 * 1024 * 1024

N_HEADS = 16
N_SLABS = 8


def _vmem_limit(nbytes):
    return int(min(max(nbytes * 3 // 2, 16 * 1024 * 1024), V7X_VMEM_BYTES * 7 // 8))


def _params(semantics, nbytes):
    return pltpu.CompilerParams(dimension_semantics=semantics, vmem_limit_bytes=_vmem_limit(nbytes))


def _dot(a, b):
    return jnp.dot(a, b, preferred_element_type=F32)


def _rmsnorm(x, g):
    ms = jnp.mean(x * x, axis=-1, keepdims=True)
    return x * lax.rsqrt(ms + EPS) * g


def _sigmoid(x):
    return 1.0 / (1.0 + jnp.exp(-x))


def _to_time_major(src, dst_ref, row0):
    bsz, t_len, ch = src.shape
    for b in range(bsz):
        for s in range(ch // V7X_LANES):
            dst_ref[s, pl.ds(row0 + b, t_len, stride=bsz), :] = src[b, :, s * V7X_LANES:(s + 1) * V7X_LANES]


def _from_time_major(src_ref, o_ref):
    bsz, t_len, ch = o_ref.shape
    for b in range(bsz):
        for s in range(ch // V7X_LANES):
            o_ref[b, :, s * V7X_LANES:(s + 1) * V7X_LANES] = (
                src_ref[s, pl.ds(b, t_len, stride=bsz), :].astype(o_ref.dtype))


CAST_TR = 512
CAST_TC_MAX = 4096


def _cast_kernel(x_ref, o_ref):
    o_ref[...] = x_ref[...].astype(o_ref.dtype)


def _to_bf16(w3):
    n, r, c = w3.shape
    tr = min(r, CAST_TR)
    tc = max(t for t in range(V7X_LANES, min(c, CAST_TC_MAX) + 1, V7X_LANES) if c % t == 0)
    assert r % tr == 0
    spec = pl.BlockSpec((None, tr, tc), lambda l, i, j: (l, i, j))
    return pl.pallas_call(
        _cast_kernel,
        grid=(n, r // tr, c // tc),
        in_specs=[spec],
        out_specs=spec,
        out_shape=jax.ShapeDtypeStruct(w3.shape, BF16),
        compiler_params=_params(("parallel", "parallel", "parallel"), 2 * tr * tc * 6),
        name="cast",
    )(w3)


def _zoh_kernel(lr_ref, li_ref, ls_ref, br_ref, bi_ref, are_ref, aim_ref, bbr_ref, bbi_ref):
    lr = lr_ref[...]
    li = li_ref[...]
    step = jnp.exp(ls_ref[...])
    mag = jnp.exp(lr * step)
    lbar_re = mag * jnp.cos(li * step)
    lbar_im = mag * jnp.sin(li * step)
    num_re = lbar_re - 1.0
    num_im = lbar_im
    den = lr * lr + li * li
    f_re = (num_re * lr + num_im * li) / den
    f_im = (num_im * lr - num_re * li) / den
    br = br_ref[...]
    bi = bi_ref[...]
    are_ref[...] = lbar_re
    aim_ref[...] = lbar_im
    bbr_ref[...] = f_re * br - f_im * bi
    bbi_ref[...] = f_re * bi + f_im * br


def _zoh(lam_re, lam_im, log_step, b_re, b_im):
    g, p = lam_re.shape
    h = b_re.shape[-1]
    rep = lambda a: jnp.broadcast_to(a[:, None, :], (g, h, p)).reshape(g * h, p)
    args = (rep(lam_re), rep(lam_im), rep(jnp.broadcast_to(log_step[:, None], (g, p))),
            jnp.transpose(b_re, (0, 2, 1)).reshape(g * h, p),
            jnp.transpose(b_im, (0, 2, 1)).reshape(g * h, p))
    out = pl.pallas_call(
        _zoh_kernel,
        out_shape=[jax.ShapeDtypeStruct((g * h, p), F32)] * 4,
        name="zoh",
    )(*args)
    are, aim, bbr, bbi = [o.reshape(g, h, p) for o in out]
    return are[:, 0], aim[:, 0], bbr, bbi


def _norm_matmul_kernel(x_ref, g_ref, w_ref, *rest, has_scale):
    if has_scale:
        s_ref, o_ref, h_ref = rest
    else:
        o_ref, h_ref = rest

    @pl.when(pl.program_id(1) == 0)
    def _():
        h_ref[...] = _rmsnorm(x_ref[...], g_ref[...]).astype(BF16)

    acc = _dot(h_ref[...], w_ref[...])
    if has_scale:
        acc = acc * s_ref[...]
    o_ref[...] = acc.astype(o_ref.dtype)


def _norm_matmul(x2d, gain, w, out_dtype, *, tm, tn, col_scale=None, name):
    n_rows = x2d.shape[0]
    _, d, n = w.shape
    in_specs = [
        pl.BlockSpec((tm, d), lambda i, j: (i, 0)),
        pl.BlockSpec((1, d), lambda i, j: (0, 0)),
        pl.BlockSpec((None, d, tn), lambda i, j: (0, 0, j)),
    ]
    args = [x2d, gain.reshape(1, d), w]
    if col_scale is not None:
        in_specs.append(pl.BlockSpec((1, tn), lambda i, j: (0, j)))
        args.append(col_scale.reshape(1, n))
    out_bytes = jnp.dtype(out_dtype).itemsize
    nbytes = 2 * tm * d * 4 + tm * d * 2 + 2 * d * tn * 2 + 2 * tm * tn * out_bytes + tm * tn * 4
    return pl.pallas_call(
        functools.partial(_norm_matmul_kernel, has_scale=col_scale is not None),
        grid=(n_rows // tm, n // tn),
        in_specs=in_specs,
        out_specs=pl.BlockSpec((tm, tn), lambda i, j: (i, j)),
        out_shape=jax.ShapeDtypeStruct((n_rows, n), out_dtype),
        scratch_shapes=[pltpu.VMEM((tm, d), BF16)],
        compiler_params=_params(("parallel", "arbitrary"), nbytes),
        name=name,
    )(*args)


def _res_matmul_kernel(*refs, n_a):
    a_refs = refs[:n_a]
    w_refs = refs[n_a:2 * n_a]
    res_ref, o_ref = refs[2 * n_a:]
    acc = res_ref[...]
    for a_ref, w_ref in zip(a_refs, w_refs):
        acc = acc + _dot(a_ref[...], w_ref[...])
    o_ref[...] = acc


def _res_matmul(a_list, w, res2d, *, tm, tn, name):
    n_a = len(a_list)
    n_rows, ka = a_list[0].shape
    assert n_a * ka == w.shape[1]
    n = w.shape[2]
    in_specs = [pl.BlockSpec((tm, ka), lambda i, j: (i, 0)) for _ in a_list]
    in_specs += [pl.BlockSpec((None, ka, tn), lambda i, j, r=r: (0, r, j)) for r in range(n_a)]
    in_specs += [pl.BlockSpec((tm, tn), lambda i, j: (i, j))]
    nbytes = 2 * tm * n_a * ka * 2 + 2 * n_a * ka * tn * 2 + 4 * tm * tn * 4 + tm * tn * 4
    return pl.pallas_call(
        functools.partial(_res_matmul_kernel, n_a=n_a),
        grid=(n_rows // tm, n // tn),
        in_specs=in_specs,
        out_specs=pl.BlockSpec((tm, tn), lambda i, j: (i, j)),
        out_shape=jax.ShapeDtypeStruct((n_rows, n), F32),
        compiler_params=_params(("parallel", "arbitrary"), nbytes),
        name=name,
    )(*a_list, *([w] * n_a), res2d)


CONV_T = 128
CONV_HALO = 32
CONV_TB = 16


def _conformer_kernel(val_ref, gate_ref, hval_ref, hgate_ref, wb_ref, cb_ref, lng_ref, lnb_ref, o_ref,
                      hx_ref, c_ref, *, taps):
    bsz, t_len, ch = val_ref.shape
    n_slab = ch // V7X_LANES
    keep = (pl.program_id(0) > 0).astype(F32)
    _to_time_major(hval_ref[...] * _sigmoid(hgate_ref[...]) * keep, hx_ref, 0)
    _to_time_major(val_ref[...] * _sigmoid(gate_ref[...]), hx_ref, CONV_HALO * bsz)

    lead = CONV_HALO - (taps - 1)
    for s in range(n_slab):
        def body(tb, carry, s=s):
            t0 = tb * CONV_TB
            acc = jnp.zeros((CONV_TB, bsz, V7X_LANES), F32)
            for k in range(taps):
                r0 = pl.multiple_of((t0 + lead + k) * bsz, bsz)
                rows = hx_ref[s, pl.ds(r0, CONV_TB * bsz), :].reshape(CONV_TB, bsz, V7X_LANES)
                acc = acc + wb_ref[k, :, s * V7X_LANES:(s + 1) * V7X_LANES][None] * rows
            c_ref[s, pl.ds(pl.multiple_of(t0 * bsz, bsz), CONV_TB * bsz), :] = acc.reshape(CONV_TB * bsz, V7X_LANES)
            return carry
        lax.fori_loop(0, t_len // CONV_TB, body, 0)

    h = jnp.concatenate([c_ref[s] for s in range(n_slab)], axis=-1) + cb_ref[...]
    mu = jnp.mean(h, axis=-1, keepdims=True)
    dlt = h - mu
    var = jnp.mean(dlt * dlt, axis=-1, keepdims=True)
    y = dlt * lax.rsqrt(var + EPS) * lng_ref[...] + lnb_ref[...]
    y = y * _sigmoid(y)
    for s in range(n_slab):
        c_ref[s] = y[:, s * V7X_LANES:(s + 1) * V7X_LANES]
    _from_time_major(c_ref, o_ref)


def _conformer(u3, conv_w, conv_b, ln_g, ln_b):
    bsz, seq, _ = u3.shape
    taps, ch = conv_w.shape
    assert taps - 1 <= CONV_HALO and CONV_T % CONV_HALO == 0 and seq % CONV_T == 0 and ch % V7X_LANES == 0
    wb = jnp.broadcast_to(conv_w[:, None, :], (taps, bsz, ch))
    ratio = CONV_T // CONV_HALO
    halo_idx = lambda c: (lambda t: (0, jnp.maximum(t * ratio - 1, 0), c))
    row = lambda a: a.reshape(1, ch)
    full2 = lambda t: (0, 0)
    nbytes = (2 * 2 * (CONV_T + CONV_HALO) * bsz * ch * 4 + 2 * taps * bsz * ch * 4
              + (2 * CONV_T + CONV_HALO) * bsz * ch * 4 + 2 * CONV_T * bsz * ch * 2 + 6 * CONV_T * bsz * ch * 4)
    return pl.pallas_call(
        functools.partial(_conformer_kernel, taps=taps),
        grid=(seq // CONV_T,),
        in_specs=[
            pl.BlockSpec((bsz, CONV_T, ch), lambda t: (0, t, 0)),
            pl.BlockSpec((bsz, CONV_T, ch), lambda t: (0, t, 1)),
            pl.BlockSpec((bsz, CONV_HALO, ch), halo_idx(0)),
            pl.BlockSpec((bsz, CONV_HALO, ch), halo_idx(1)),
            pl.BlockSpec((taps, bsz, ch), lambda t: (0, 0, 0)),
            pl.BlockSpec((1, ch), full2),
            pl.BlockSpec((1, ch), full2),
            pl.BlockSpec((1, ch), full2),
        ],
        out_specs=pl.BlockSpec((bsz, CONV_T, ch), lambda t: (0, t, 0)),
        out_shape=jax.ShapeDtypeStruct((bsz, seq, ch), BF16),
        scratch_shapes=[pltpu.VMEM((ch // V7X_LANES, (CONV_HALO + CONV_T) * bsz, V7X_LANES), F32),
                        pltpu.VMEM((ch // V7X_LANES, CONV_T * bsz, V7X_LANES), F32)],
        compiler_params=_params(("parallel",), nbytes),
        name="conformer",
    )(u3, u3, u3, u3, wb, row(conv_b), row(ln_g), row(ln_b))


S5_T = 32


def _s5_kernel(u_ref, wb_ref, lbar_ref, wcr_ref, wci_ref, d_ref, wg_ref, bg_ref, o_ref, ut_ref, bu_ref, st_ref):
    bsz, t_len, ch = u_ref.shape
    rows = t_len * bsz
    n_slabs, _, width = lbar_ref.shape
    half = width // 2
    cin = wb_ref.shape[1]

    @pl.when(pl.program_id(0) == 0)
    def _():
        st_ref[...] = jnp.zeros(st_ref.shape, F32)

    _to_time_major(u_ref[...], ut_ref, 0)
    u = jnp.concatenate([ut_ref[s] for s in range(ch // V7X_LANES)], axis=-1)
    ub = u.astype(BF16)
    for j in range(n_slabs):
        bu_ref[j] = _dot(ub[:, j * cin:(j + 1) * cin], wb_ref[j])

    ys = []
    for j in range(n_slabs):
        a = lbar_ref[j]
        ar, ai = a[:, :half], a[:, half:]
        s = st_ref[j]
        xr, xi = s[:, :half], s[:, half:]
        for t in range(t_len):
            b = bu_ref[j, t * bsz:(t + 1) * bsz, :]
            xr, xi = ar * xr - ai * xi + b[:, :half], ar * xi + ai * xr + b[:, half:]
            bu_ref[j, t * bsz:(t + 1) * bsz, :] = jnp.concatenate([xr, xi], axis=-1)
        st_ref[j] = jnp.concatenate([xr, xi], axis=-1)
        s = bu_ref[j]
        ys.append(_dot(s[:, :half].astype(BF16), wcr_ref[j]) - _dot(s[:, half:].astype(BF16), wci_ref[j]))
    y = jnp.concatenate(ys, axis=-1) + d_ref[...] * u
    y = jax.nn.gelu(y)
    gate = _sigmoid(_dot(y.astype(BF16), wg_ref[...]) + bg_ref[...])
    y = y * gate
    for s in range(ch // V7X_LANES):
        ut_ref[s] = y[:, s * V7X_LANES:(s + 1) * V7X_LANES]
    _from_time_major(ut_ref, o_ref)


def _s5(u3, cblk, lbar_re, lbar_im, bb_re, bb_im, c_re, c_im, d_skip, w_glu, b_glu):
    bsz, seq, _ = u3.shape
    g, h, p = bb_re.shape
    ch = g * h
    gs = g // N_SLABS
    half = gs * p
    eye = jnp.eye(gs, dtype=F32)

    def place_b(bb):
        return jnp.einsum("jghp,gk->jghkp", bb.reshape(N_SLABS, gs, h, p), eye).reshape(N_SLABS, gs * h, half)

    def place_c(c):
        return jnp.einsum("jghp,gk->jgpkh", c.reshape(N_SLABS, gs, h, p), eye).reshape(N_SLABS, half, gs * h)

    wb = jnp.concatenate([place_b(bb_re), place_b(bb_im)], axis=-1).astype(BF16)
    wcr = place_c(c_re).astype(BF16)
    wci = place_c(c_im).astype(BF16)
    lbar = jnp.concatenate([lbar_re.reshape(N_SLABS, half), lbar_im.reshape(N_SLABS, half)], axis=-1)
    lbar = jnp.broadcast_to(lbar[:, None, :], (N_SLABS, bsz, 2 * half))
    rows = S5_T * bsz
    full2 = lambda t: (0, 0)
    full3 = lambda t: (0, 0, 0)
    nbytes = (2 * rows * ch * 4 + 2 * rows * ch * 2 + rows * ch * 4 + N_SLABS * rows * 2 * half * 4
              + 2 * (wb.size + wcr.size + wci.size + w_glu.size) * 2 + 2 * lbar.size * 4 + 6 * rows * ch * 4)
    return pl.pallas_call(
        _s5_kernel,
        grid=(seq // S5_T,),
        in_specs=[
            pl.BlockSpec((bsz, S5_T, ch), lambda t: (0, t, cblk)),
            pl.BlockSpec(wb.shape, full3),
            pl.BlockSpec(lbar.shape, full3),
            pl.BlockSpec(wcr.shape, full3),
            pl.BlockSpec(wci.shape, full3),
            pl.BlockSpec((1, ch), full2),
            pl.BlockSpec((None,) + w_glu.shape[1:], full3),
            pl.BlockSpec((1, ch), full2),
        ],
        out_specs=pl.BlockSpec((bsz, S5_T, ch), lambda t: (0, t, 0)),
        out_shape=jax.ShapeDtypeStruct((bsz, seq, ch), BF16),
        scratch_shapes=[pltpu.VMEM((ch // V7X_LANES, rows, V7X_LANES), F32),
                        pltpu.VMEM((N_SLABS, rows, 2 * half), F32),
                        pltpu.VMEM((N_SLABS, bsz, 2 * half), F32)],
        compiler_params=_params(("arbitrary",), nbytes),
        name="s5",
    )(u3, wb, lbar, wcr, wci, d_skip.reshape(1, ch), w_glu, b_glu.reshape(1, ch))


FFN_T = 64
FFN_TC = 512


def _ffn_kernel(x_ref, g_ref, wg_ref, wv_ref, cw_ref, wd_ref, *rest, taps, final):
    if final:
        fg_ref, o_ref, acc_ref, h_ref, carry_ref = rest
    else:
        o_ref, acc_ref, h_ref, carry_ref = rest
    bsz, t_len, d = x_ref.shape
    tm = bsz * t_len
    n_slab = d // V7X_LANES
    hist = (taps - 1) * bsz
    c = pl.program_id(1)

    def full_rows(ref):
        return jnp.concatenate([ref[s] for s in range(n_slab)], axis=-1)

    @pl.when(jnp.logical_and(pl.program_id(0) == 0, c == 0))
    def _():
        carry_ref[...] = jnp.zeros(carry_ref.shape, F32)

    @pl.when(c == 0)
    def _():
        _to_time_major(x_ref[...], acc_ref, 0)
        h_ref[...] = _rmsnorm(full_rows(acc_ref), g_ref[...]).astype(BF16)

    h = h_ref[...]

    def conv(u, cw, slot):
        ext = jnp.concatenate([carry_ref[c, slot], u], axis=0)
        carry_ref[c, slot] = u[tm - hist:tm]
        out = cw[0:1, :] * ext[0:tm]
        for k in range(1, taps):
            out = out + cw[k:k + 1, :] * ext[k * bsz:k * bsz + tm]
        return out

    n_c = pl.num_programs(1)
    gate = conv(_dot(h, wg_ref[...]), cw_ref[c], 0)
    val = conv(_dot(h, wv_ref[...]), cw_ref[n_c + c], 1)
    act = (gate * _sigmoid(gate) * val).astype(BF16)
    part = _dot(act, wd_ref[...])
    for s in range(n_slab):
        acc_ref[s] += part[:, s * V7X_LANES:(s + 1) * V7X_LANES]

    @pl.when(c == pl.num_programs(1) - 1)
    def _():
        if final:
            y = _rmsnorm(full_rows(acc_ref), fg_ref[...])
            for s in range(n_slab):
                acc_ref[s] = y[:, s * V7X_LANES:(s + 1) * V7X_LANES]
        _from_time_major(acc_ref, o_ref)


def _ffn(x3, gain, w_up, conv_w, w_down, layer, final_gain=None):
    bsz, seq, d = x3.shape
    taps = conv_w.shape[0]
    d_ff = w_down.shape[1]
    assert (taps - 1 <= FFN_T and seq % FFN_T == 0 and d_ff % FFN_TC == 0
            and bsz % V7X_SUBLANES == 0 and d % V7X_LANES == 0)
    nc = d_ff // FFN_TC
    tm = FFN_T * bsz
    hist = (taps - 1) * bsz
    final = final_gain is not None
    in_specs = [
        pl.BlockSpec((bsz, FFN_T, d), lambda i, c: (0, i, 0)),
        pl.BlockSpec((1, d), lambda i, c: (0, 0)),
        pl.BlockSpec((None, d, FFN_TC), lambda i, c: (layer, 0, c)),
        pl.BlockSpec((None, d, FFN_TC), lambda i, c: (layer, 0, nc + c)),
        pl.BlockSpec((2 * nc, taps, FFN_TC), lambda i, c: (0, 0, 0)),
        pl.BlockSpec((None, FFN_TC, d), lambda i, c: (layer, c, 0)),
    ]
    cw = jnp.transpose(conv_w.reshape(taps, 2 * nc, FFN_TC), (1, 0, 2))
    args = [x3, gain.reshape(1, d), w_up, w_up, cw, w_down]
    if final:
        in_specs.append(pl.BlockSpec((1, d), lambda i, c: (0, 0)))
        args.append(final_gain.reshape(1, d))
    nbytes = (5 * tm * d * 4 + tm * d * 2 + 2 * 3 * d * FFN_TC * 2 + nc * 2 * hist * FFN_TC * 4
              + 6 * (tm + hist) * FFN_TC * 4)
    return pl.pallas_call(
        functools.partial(_ffn_kernel, taps=taps, final=final),
        grid=(seq // FFN_T, nc),
        in_specs=in_specs,
        out_specs=pl.BlockSpec((bsz, FFN_T, d), lambda i, c: (0, i, 0)),
        out_shape=jax.ShapeDtypeStruct((bsz, seq, d), F32),
        scratch_shapes=[pltpu.VMEM((d // V7X_LANES, tm, V7X_LANES), F32),
                        pltpu.VMEM((tm, d), BF16),
                        pltpu.VMEM((nc, 2, hist, FFN_TC), F32)],
        compiler_params=_params(("arbitrary", "arbitrary"), nbytes),
        name="ffn",
    )(*args)


ATT_T = 256
ATT_G = 4
LOG2_E = 1.4426950408889634
F32_EXP2_UNDERFLOW = -150.0


def _attn_kernel(q_ref, k_ref, v_ref, tri_ref, o_ref, acc_ref, carry_ref):
    blk = tri_ref.shape[0]
    n_g = acc_ref.shape[0]
    dh = acc_ref.shape[1]
    n_q = q_ref.shape[0] // blk
    tri = tri_ref[...]

    def rows(kb):
        return pl.ds(pl.multiple_of(kb * blk, blk), blk)

    def process(qi, chains):
        cols = lambda g: slice(g * dh, (g + 1) * dh)
        zs = [lax.dot_general(k_ref[rows(kb), cols(g)], q_ref[rows(qi), cols(g)], (((1,), (1,)), ((), ())),
                              preferred_element_type=F32) for g, kb, _ in chains]
        before = (lax.broadcasted_iota(jnp.int32, (blk, blk), 0) < lax.broadcasted_iota(jnp.int32, (blk, blk), 1))
        parts = []
        for (_, _, diagonal), z in zip(chains, zs):
            log_beta = jnp.minimum(z, 0.0) - jnp.log(1.0 + jnp.exp2(-jnp.abs(z))) * LOG2_E
            log_keep = log_beta - z
            if diagonal:
                log_keep = jnp.where(before, log_keep, 0.0)
            parts.append((log_beta, log_keep))
        sufs = [_dot(tri, log_keep.astype(BF16)) for _, log_keep in parts]
        pvs = []
        for (g, kb, diagonal), (log_beta, _), suffix in zip(chains, parts, sufs):
            w = jnp.exp2(log_beta + suffix)
            if diagonal:
                w = jnp.where(before, w, 0.0)
            pvs.append(lax.dot_general(v_ref[rows(kb), cols(g)], w.astype(BF16), (((0,), (0,)), ((), ())),
                                       preferred_element_type=F32))
        for (g, _, _), (_, log_keep), pv in zip(chains, parts, pvs):
            carry = carry_ref[g]
            acc_ref[g] += jnp.exp2(carry) * pv
            carry_ref[g] = carry + jnp.sum(log_keep, axis=0, keepdims=True)

    def reset():
        acc_ref[...] = jnp.zeros(acc_ref.shape, F32)
        carry_ref[...] = jnp.zeros(carry_ref.shape, F32)

    def emit(qi):
        for g in range(n_g):
            o_ref[rows(qi), g * dh:(g + 1) * dh] = acc_ref[g].T.astype(o_ref.dtype)

    reset()
    process(0, [(g, 0, True) for g in range(n_g)])
    emit(0)

    def query_block(qi, state):
        reset()
        process(qi, [(g, qi, True) for g in range(n_g)] + [(g, qi - 1, False) for g in range(n_g)])

        def cond(inner):
            kb, max_carry = inner
            return jnp.logical_and(kb >= 0, max_carry > F32_EXP2_UNDERFLOW)

        def body(inner):
            kb, _ = inner
            process(qi, [(g, kb, False) for g in range(n_g)])
            return kb - 1, jnp.max(carry_ref[...])

        lax.while_loop(cond, body, (qi - 2, jnp.max(carry_ref[...])))
        emit(qi)
        return state
    lax.fori_loop(1, n_q, query_block, 0)


def _attention(qkv, bsz, seq):
    d = qkv.shape[1] // 3
    dh = d // N_HEADS
    assert seq % ATT_T == 0 and N_HEADS % ATT_G == 0
    hg = N_HEADS // ATT_G
    gw = ATT_G * dh
    idx = jnp.arange(ATT_T)
    tri = (idx[None, :] > idx[:, None]).astype(BF16)
    nbytes = (2 * 4 * seq * gw * 2 + 2 * ATT_T * ATT_T * 2
              + ATT_G * (dh + V7X_SUBLANES) * ATT_T * 4 + 2 * ATT_G * 10 * ATT_T * ATT_T * 4)
    return pl.pallas_call(
        _attn_kernel,
        grid=(bsz, hg),
        in_specs=[
            pl.BlockSpec((seq, gw), lambda b, h: (b, h)),
            pl.BlockSpec((seq, gw), lambda b, h: (b, hg + h)),
            pl.BlockSpec((seq, gw), lambda b, h: (b, 2 * hg + h)),
            pl.BlockSpec((ATT_T, ATT_T), lambda b, h: (0, 0)),
        ],
        out_specs=pl.BlockSpec((seq, gw), lambda b, h: (b, h)),
        out_shape=jax.ShapeDtypeStruct((bsz * seq, d), BF16),
        scratch_shapes=[pltpu.VMEM((ATT_G, dh, ATT_T), F32), pltpu.VMEM((ATT_G, 1, ATT_T), F32)],
        compiler_params=_params(("parallel", "parallel"), nbytes),
        name="attention",
    )(qkv, qkv, qkv, tri)


PROJ_TM = 1024
PROJ_TN = 1024


def kernel(x, ln_mix_even, w_in, conv_w, conv_b, conv_ln_g, conv_ln_b, ssm_lam_re, ssm_lam_im, ssm_log_step, ssm_b_re, ssm_b_im, ssm_c_re, ssm_c_im, ssm_d, ssm_w_glu, ssm_b_glu, w_out_even, ln_mix_odd, w_qkv, w_o, ln_ffn, ffn_w_up, ffn_conv_w, ffn_w_down, ln_final):
    bsz, seq, d = x.shape
    depth = ln_ffn.shape[0]
    assert depth == 2 and ln_mix_even.shape[0] == 1 and ln_mix_odd.shape[0] == 1
    tm = min(PROJ_TM, seq)
    tn = min(PROJ_TN, d)
    d_conv = conv_w.shape[-1]
    d_ssm = ssm_w_glu.shape[-1]
    d_in = w_in.shape[-1]
    assert d_conv == d_ssm and d_in == 2 * d_conv + d_ssm
    x2d = x.reshape(bsz * seq, d)
    w_up_bf = _to_bf16(ffn_w_up)
    w_down_bf = _to_bf16(ffn_w_down)

    u = _norm_matmul(x2d, ln_mix_even[0], _to_bf16(w_in), F32, tm=tm, tn=tn, name="in_proj")
    u3 = u.reshape(bsz, seq, d_in)
    ya = _conformer(u3, conv_w[0], conv_b[0], conv_ln_g[0], conv_ln_b[0])
    lbar_re, lbar_im, bb_re, bb_im = _zoh(ssm_lam_re[0], ssm_lam_im[0], ssm_log_step[0], ssm_b_re[0], ssm_b_im[0])
    yb = _s5(u3, 2, lbar_re, lbar_im, bb_re, bb_im, ssm_c_re[0], ssm_c_im[0], ssm_d[0],
             _to_bf16(ssm_w_glu), ssm_b_glu[0])
    x2d = _res_matmul([ya.reshape(bsz * seq, d_conv), yb.reshape(bsz * seq, d_ssm)], _to_bf16(w_out_even),
                      x2d, tm=tm, tn=tn, name="out_proj")
    x2d = _ffn(x2d.reshape(bsz, seq, d), ln_ffn[0], w_up_bf, ffn_conv_w[0], w_down_bf, 0).reshape(bsz * seq, d)

    dh = d // N_HEADS
    col_scale = jnp.concatenate([jnp.full((d,), dh ** -0.5 * LOG2_E, F32), jnp.ones((2 * d,), F32)])
    qkv = _norm_matmul(x2d, ln_mix_odd[0], _to_bf16(w_qkv), BF16, tm=tm, tn=tn, col_scale=col_scale,
                       name="qkv_proj")
    o = _attention(qkv, bsz, seq)
    x2d = _res_matmul([o], _to_bf16(w_o), x2d, tm=tm, tn=tn, name="o_proj")
    return _ffn(x2d.reshape(bsz, seq, d), ln_ffn[1], w_up_bf, ffn_conv_w[1], w_down_bf, 1, final_gain=ln_final)
```
